```python
import math, functools
import jax, jax.numpy as jnp
from jax import lax
import numpy as np

D_MODEL = 2048
BATCH = 4
SEQ = 4096
DEPTH = 1
DEC_BATCH = 32
DEC_SEQ = 1
PAST_LEN = 16384
PAGE_SIZE = 128

HEAD_DIM = 128
H_A = D_MODEL // 256
H_B = D_MODEL // 256
W_A = H_A * HEAD_DIM
W_B = H_B * HEAD_DIM
MOBA_BLOCK = 256
MOBA_TOPK = 3
MOBA_Q_BLOCK = 64
DSA_TOPK = 256
DSA_Q_BLOCK = 128
IDX_HEADS = D_MODEL // 128
IDX_DIM = 64
N_BUCKETS = 32
REL_MAX_DIST = 4096
PEER_HEADS = 8
PEER_NKEYS = 128
PEER_N = PEER_NKEYS * PEER_NKEYS
PEER_DKEY = 256
PEER_TOPK = 16
PEER_TOK_BLOCK = 128
IN_WIDTH = 3 * W_A + 3 * W_B + IDX_HEADS * IDX_DIM + IDX_DIM + IDX_HEADS + 2 * D_MODEL
RMS_EPS = 1e-6
NEG = -1e30

kernel_name = 'moba_dsa_peer_adaln_hybrid'


def rms_norm(x, g):
    xf = x.astype(jnp.float32)
    y = xf * lax.rsqrt(jnp.mean(xf * xf, axis=-1, keepdims=True) + RMS_EPS)
    return (y * g.astype(jnp.float32)).astype(x.dtype)


def rel_bucket(dist):
    n = jnp.maximum(dist, 0)
    max_exact = N_BUCKETS // 2
    nf = jnp.maximum(n, 1).astype(jnp.float32)
    large = max_exact + (jnp.log(nf / max_exact) / math.log(REL_MAX_DIST / max_exact)
                         * (N_BUCKETS - max_exact)).astype(jnp.int32)
    large = jnp.minimum(large, N_BUCKETS - 1)
    return jnp.where(n < max_exact, n, large)


def ada_mod(c, w_ada, b_ada):
    m = jax.nn.silu(c) @ w_ada + b_ada
    return tuple(z[:, None, :] for z in jnp.split(m, 6, axis=-1))


def split_proj(z):
    sizes = (W_A, W_A, W_A, W_B, W_B, W_B, IDX_HEADS * IDX_DIM, IDX_DIM, IDX_HEADS, D_MODEL, D_MODEL)
    offsets = [int(o) for o in np.cumsum(sizes)[:-1]]
    return jnp.split(z, offsets, axis=-1)


def moba_prepare(k, v):
    L = k.shape[0]
    n_blk = -(-L // MOBA_BLOCK)
    pad = n_blk * MOBA_BLOCK - L
    k = jnp.pad(k, ((0, pad), (0, 0), (0, 0)))
    v = jnp.pad(v, ((0, pad), (0, 0), (0, 0)))
    kb = k.reshape(n_blk, MOBA_BLOCK, H_A, HEAD_DIM).transpose(2, 0, 1, 3)
    vb = v.reshape(n_blk, MOBA_BLOCK, H_A, HEAD_DIM).transpose(2, 0, 1, 3)
    k_mean = jnp.mean(kb.astype(jnp.float32), axis=2)
    return kb, vb, k_mean


def moba_query(q, q_pos, kb, vb, k_mean, bias_ht):
    tq = q.shape[0]
    n_blk = kb.shape[1]
    n_sel = min(MOBA_TOPK, n_blk)
    own = q_pos // MOBA_BLOCK
    qf = q.astype(jnp.float32)
    gate = jnp.einsum('qhd,hbd->qhb', qf, k_mean)
    past = jnp.arange(n_blk)[None, None, :] < own[:, None, None]
    gate = jnp.where(past, gate, NEG)
    _, sel = lax.top_k(gate, n_sel)
    sel_ok = sel < own[:, None, None]
    blocks = jnp.concatenate([sel, jnp.broadcast_to(own[:, None, None], (tq, H_A, 1))], axis=-1)
    blk_ok = jnp.concatenate([sel_ok, jnp.ones((tq, H_A, 1), dtype=bool)], axis=-1)
    h_idx = jnp.arange(H_A)[None, :, None]
    k_ctx = kb[h_idx, blocks]
    v_ctx = vb[h_idx, blocks]
    k_pos = blocks[..., None] * MOBA_BLOCK + jnp.arange(MOBA_BLOCK)
    dist = q_pos[:, None, None, None] - k_pos
    ok = blk_ok[..., None] & (dist >= 0)
    logits = (jnp.einsum('qhd,qhnkd->qhnk', qf, k_ctx.astype(jnp.float32)) * HEAD_DIM ** -0.5
              + bias_ht[h_idx[..., None], rel_bucket(dist)].astype(jnp.float32))
    logits = jnp.where(ok, logits, NEG).reshape(tq, H_A, -1)
    p = jax.nn.softmax(logits, axis=-1)
    out = jnp.einsum('qhk,qhkd->qhd', p, v_ctx.reshape(tq, H_A, -1, HEAD_DIM).astype(jnp.float32))
    return out.astype(q.dtype)


def moba_prompt_seq(args, bias_ht):
    q, k, v = args
    t = q.shape[0]
    nc = t // MOBA_Q_BLOCK
    kb, vb, km = moba_prepare(k, v)
    qc = q.reshape(nc, MOBA_Q_BLOCK, H_A, HEAD_DIM)
    pc = jnp.arange(t, dtype=jnp.int32).reshape(nc, MOBA_Q_BLOCK)
    out = lax.map(lambda a: moba_query(a[0], a[1], kb, vb, km, bias_ht), (qc, pc))
    return out.reshape(t, H_A, HEAD_DIM)


def moba_sample_seq(args, cache_k, cache_v, bias_ht):
    q, k_new, v_new, pt = args
    past = pt.shape[0] * PAGE_SIZE
    k = jnp.concatenate([cache_k[pt].reshape(past, H_A, HEAD_DIM), k_new.astype(cache_k.dtype)], axis=0)
    v = jnp.concatenate([cache_v[pt].reshape(past, H_A, HEAD_DIM), v_new.astype(cache_v.dtype)], axis=0)
    kb, vb, km = moba_prepare(k, v)
    q_pos = past + jnp.arange(q.shape[0], dtype=jnp.int32)
    return moba_query(q, q_pos, kb, vb, km, bias_ht)


def dsa_query(q, q_pos, qi, wi, k, v, kidx, bias_b):
    L = k.shape[0]
    n_sel = min(DSA_TOPK, L // 4)
    s = jnp.einsum('qhd,sd->qhs', qi.astype(jnp.float32), kidx.astype(jnp.float32)) * IDX_DIM ** -0.5
    score = jnp.einsum('qh,qhs->qs', wi.astype(jnp.float32) * IDX_HEADS ** -0.5, jax.nn.relu(s))
    score = jnp.where(jnp.arange(L)[None, :] <= q_pos[:, None], score, NEG)
    _, sel = lax.top_k(score, n_sel)
    k_sel = k[sel]
    v_sel = v[sel]
    dist = q_pos[:, None] - sel
    qf = q.astype(jnp.float32)
    logits = (jnp.einsum('qhd,qkhd->qhk', qf, k_sel.astype(jnp.float32)) * HEAD_DIM ** -0.5
              + bias_b[rel_bucket(dist)].astype(jnp.float32).transpose(0, 2, 1))
    logits = jnp.where((dist >= 0)[:, None, :], logits, NEG)
    p = jax.nn.softmax(logits, axis=-1)
    out = jnp.einsum('qhk,qkhd->qhd', p, v_sel.astype(jnp.float32))
    return out.astype(q.dtype)


def dsa_prompt_seq(args, bias_b):
    q, k, v, qi, ki, wi = args
    t = q.shape[0]
    nc = t // DSA_Q_BLOCK
    qc = q.reshape(nc, DSA_Q_BLOCK, H_B, HEAD_DIM)
    qic = qi.reshape(nc, DSA_Q_BLOCK, IDX_HEADS, IDX_DIM)
    wic = wi.reshape(nc, DSA_Q_BLOCK, IDX_HEADS)
    pc = jnp.arange(t, dtype=jnp.int32).reshape(nc, DSA_Q_BLOCK)
    out = lax.map(lambda a: dsa_query(a[0], a[1], a[2], a[3], k, v, ki, bias_b), (qc, pc, qic, wic))
    return out.reshape(t, H_B, HEAD_DIM)


def dsa_sample_seq(args, cache_k, cache_v, cache_ki, bias_b):
    q, k_new, v_new, qi, ki_new, wi, pt = args
    past = pt.shape[0] * PAGE_SIZE
    k = jnp.concatenate([cache_k[pt].reshape(past, H_B, HEAD_DIM), k_new.astype(cache_k.dtype)], axis=0)
    v = jnp.concatenate([cache_v[pt].reshape(past, H_B, HEAD_DIM), v_new.astype(cache_v.dtype)], axis=0)
    ki = jnp.concatenate([cache_ki[pt].reshape(past, IDX_DIM), ki_new.astype(cache_ki.dtype)], axis=0)
    q_pos = past + jnp.arange(q.shape[0], dtype=jnp.int32)
    return dsa_query(q, q_pos, qi, wi, k, v, ki, bias_b)


def attend_prompt(qa, ka, va, qb, kb, vb, qi, ki, wi, bias_a_ht, bias_b):
    out_a = lax.map(functools.partial(moba_prompt_seq, bias_ht=bias_a_ht), (qa, ka, va))
    out_b = lax.map(functools.partial(dsa_prompt_seq, bias_b=bias_b), (qb, kb, vb, qi, ki, wi))
    return out_a, out_b


def attend_sample(qa, ka, va, qb, kb, vb, qi, ki, wi, bias_a_ht, bias_b, page_table, cka, cva, ckb, cvb, cki):
    out_a = lax.map(functools.partial(moba_sample_seq, cache_k=cka, cache_v=cva, bias_ht=bias_a_ht),
                    (qa, ka, va, page_table))
    out_b = lax.map(functools.partial(dsa_sample_seq, cache_k=ckb, cache_v=cvb, cache_ki=cki, bias_b=bias_b),
                    (qb, kb, vb, qi, ki, wi, page_table))
    return out_a, out_b


def peer_block(x, w_q, keys1, keys2, u, v):
    n = x.shape[0]
    q = (x @ w_q).reshape(n, PEER_HEADS, 2, PEER_DKEY // 2).astype(jnp.float32)
    s1 = jnp.einsum('nhd,hkd->nhk', q[:, :, 0], keys1.astype(jnp.float32))
    s2 = jnp.einsum('nhd,hkd->nhk', q[:, :, 1], keys2.astype(jnp.float32))
    v1, i1 = lax.top_k(s1, PEER_TOPK)
    v2, i2 = lax.top_k(s2, PEER_TOPK)
    cand = (v1[..., :, None] + v2[..., None, :]).reshape(n, PEER_HEADS, PEER_TOPK * PEER_TOPK)
    cid = (i1[..., :, None] * PEER_NKEYS + i2[..., None, :]).reshape(n, PEER_HEADS, PEER_TOPK * PEER_TOPK)
    sc, j = lax.top_k(cand, PEER_TOPK)
    eid = jnp.take_along_axis(cid, j, axis=-1)
    g = jax.nn.softmax(sc, axis=-1)
    act = jax.nn.gelu(jnp.einsum('nd,nhkd->nhk', x.astype(jnp.float32), u[eid].astype(jnp.float32)))
    out = jnp.einsum('nhk,nhkd->nd', g * act, v[eid].astype(jnp.float32))
    return out.astype(x.dtype)


def peer_ffn(x, w_q, keys1, keys2, u, v):
    nb, t, d = x.shape
    n = nb * t
    n_blk = -(-n // PEER_TOK_BLOCK)
    xf = jnp.pad(x.reshape(n, d), ((0, n_blk * PEER_TOK_BLOCK - n), (0, 0)))
    out = lax.map(lambda z: peer_block(z, w_q, keys1, keys2, u, v), xf.reshape(n_blk, PEER_TOK_BLOCK, d))
    return out.reshape(-1, d)[:n].reshape(nb, t, d)


def trunk_layer(x, c, attend, w_ada, b_ada, norm_mix_g, w_in, qn_a, kn_a, qn_b, kn_b,
                w_proj_a, w_proj_b, w_out, norm_ffn_g, w_q_peer, keys1, keys2, peer_u, peer_v):
    sh1, sc1, g1, sh2, sc2, g2 = ada_mod(c, w_ada, b_ada)
    nb, t, _ = x.shape
    xm = rms_norm(x, norm_mix_g) * (1 + sc1) + sh1
    qa, ka, va, qb, kb, vb, qi, ki, wi, ga, gb = split_proj(xm @ w_in)
    qa = rms_norm(qa.reshape(nb, t, H_A, HEAD_DIM), qn_a)
    ka = rms_norm(ka.reshape(nb, t, H_A, HEAD_DIM), kn_a)
    va = va.reshape(nb, t, H_A, HEAD_DIM)
    qb = rms_norm(qb.reshape(nb, t, H_B, HEAD_DIM), qn_b)
    kb = rms_norm(kb.reshape(nb, t, H_B, HEAD_DIM), kn_b)
    vb = vb.reshape(nb, t, H_B, HEAD_DIM)
    qi = qi.reshape(nb, t, IDX_HEADS, IDX_DIM)
    out_a, out_b = attend(qa, ka, va, qb, kb, vb, qi, ki, wi)
    br_a = out_a.reshape(nb, t, W_A) @ w_proj_a
    br_b = out_b.reshape(nb, t, W_B) @ w_proj_b
    merged = jax.nn.sigmoid(ga) * br_a + jax.nn.sigmoid(gb) * br_b
    h = x + g1 * (merged @ w_out)
    hm = rms_norm(h, norm_ffn_g) * (1 + sc2) + sh2
    y = h + g2 * peer_ffn(hm, w_q_peer, keys1, keys2, peer_u, peer_v)
    return y, (ka, va, kb, vb, ki)


def setup_inputs(seed: int = 0) -> dict:
    key = jax.random.key(seed)
    ks = jax.random.split(key, 32)
    n_pages = PAST_LEN // PAGE_SIZE
    n_used = DEC_BATCH * n_pages
    n_pool = n_used + n_used // 4 + 1
    f32 = jnp.float32

    def nrm(k, shape, s):
        return s * jax.random.normal(k, shape, f32)

    page_table = jax.random.permutation(ks[7], n_pool)[:n_used].reshape(DEC_BATCH, n_pages).astype(jnp.int32)
    return {
        'x_prompt': nrm(ks[0], (BATCH, SEQ, D_MODEL), 1.0),
        'x_sample': nrm(ks[1], (DEC_BATCH, DEC_SEQ, D_MODEL), 1.0),
        'cache_k_a': nrm(ks[2], (DEPTH, n_pool, PAGE_SIZE, H_A, HEAD_DIM), 1.0),
        'cache_v_a': nrm(ks[3], (DEPTH, n_pool, PAGE_SIZE, H_A, HEAD_DIM), 1.0),
        'cache_k_b': nrm(ks[4], (DEPTH, n_pool, PAGE_SIZE, H_B, HEAD_DIM), 1.0),
        'cache_v_b': nrm(ks[5], (DEPTH, n_pool, PAGE_SIZE, H_B, HEAD_DIM), 1.0),
        'cache_k_idx': nrm(ks[6], (DEPTH, n_pool, PAGE_SIZE, IDX_DIM), 1.0),
        'page_table': page_table,
        'c_prompt': nrm(ks[8], (BATCH, D_MODEL), 1.0),
        'c_sample': nrm(ks[9], (DEC_BATCH, D_MODEL), 1.0),
        'rel_bias': nrm(ks[10], (N_BUCKETS, H_A + H_B), 0.5),
        'w_ada': nrm(ks[11], (DEPTH, D_MODEL, 6 * D_MODEL), D_MODEL ** -0.5),
        'b_ada': nrm(ks[12], (DEPTH, 6 * D_MODEL), 0.02),
        'norm_mix_g': 1.0 + nrm(ks[13], (DEPTH, D_MODEL), 0.02),
        'w_in': nrm(ks[14], (DEPTH, D_MODEL, IN_WIDTH), D_MODEL ** -0.5),
        'qnorm_a': 1.0 + nrm(ks[15], (DEPTH, HEAD_DIM), 0.02),
        'knorm_a': 1.0 + nrm(ks[16], (DEPTH, HEAD_DIM), 0.02),
        'qnorm_b': 1.0 + nrm(ks[17], (DEPTH, HEAD_DIM), 0.02),
        'knorm_b': 1.0 + nrm(ks[18], (DEPTH, HEAD_DIM), 0.02),
        'w_proj_a': nrm(ks[19], (DEPTH, W_A, D_MODEL), W_A ** -0.5),
        'w_proj_b': nrm(ks[20], (DEPTH, W_B, D_MODEL), W_B ** -0.5),
        'w_out': nrm(ks[21], (DEPTH, D_MODEL, D_MODEL), D_MODEL ** -0.5),
        'norm_ffn_g': 1.0 + nrm(ks[22], (DEPTH, D_MODEL), 0.02),
        'w_q_peer': nrm(ks[23], (DEPTH, D_MODEL, PEER_HEADS * PEER_DKEY), D_MODEL ** -0.5),
        'peer_keys1': nrm(ks[24], (DEPTH, PEER_HEADS, PEER_NKEYS, PEER_DKEY // 2), (PEER_DKEY // 2) ** -0.5),
        'peer_keys2': nrm(ks[25], (DEPTH, PEER_HEADS, PEER_NKEYS, PEER_DKEY // 2), (PEER_DKEY // 2) ** -0.5),
        'peer_u': nrm(ks[26], (DEPTH, PEER_N, D_MODEL), D_MODEL ** -0.5),
        'peer_v': nrm(ks[27], (DEPTH, PEER_N, D_MODEL), PEER_HEADS ** -0.5),
    }


def reference(x_prompt, x_sample, cache_k_a, cache_v_a, cache_k_b, cache_v_b, cache_k_idx, page_table,
              c_prompt, c_sample, rel_bias, w_ada, b_ada, norm_mix_g, w_in, qnorm_a, knorm_a, qnorm_b, knorm_b,
              w_proj_a, w_proj_b, w_out, norm_ffn_g, w_q_peer, peer_keys1, peer_keys2, peer_u, peer_v):
    bias_a_ht = rel_bias[:, :H_A].T
    bias_b = rel_bias[:, H_A:]
    y_p, y_s = x_prompt, x_sample
    st_p, st_s = [], []
    for l in range(DEPTH):
        lw = (w_ada[l], b_ada[l], norm_mix_g[l], w_in[l], qnorm_a[l], knorm_a[l], qnorm_b[l], knorm_b[l],
              w_proj_a[l], w_proj_b[l], w_out[l], norm_ffn_g[l], w_q_peer[l], peer_keys1[l], peer_keys2[l],
              peer_u[l], peer_v[l])
        att_p = functools.partial(attend_prompt, bias_a_ht=bias_a_ht, bias_b=bias_b)
        att_s = functools.partial(attend_sample, bias_a_ht=bias_a_ht, bias_b=bias_b, page_table=page_table,
                                  cka=cache_k_a[l], cva=cache_v_a[l], ckb=cache_k_b[l], cvb=cache_v_b[l],
                                  cki=cache_k_idx[l])
        y_p, new_p = trunk_layer(y_p, c_prompt, att_p, *lw)
        y_s, new_s = trunk_layer(y_s, c_sample, att_s, *lw)
        st_p.append(new_p)
        st_s.append(new_s)
    ka_p = jnp.stack([s[0] for s in st_p])
    va_p = jnp.stack([s[1] for s in st_p])
    kb_p = jnp.stack([s[2] for s in st_p])
    vb_p = jnp.stack([s[3] for s in st_p])
    ki_p = jnp.stack([s[4] for s in st_p])
    ka_s = jnp.stack([s[0] for s in st_s])
    va_s = jnp.stack([s[1] for s in st_s])
    kb_s = jnp.stack([s[2] for s in st_s])
    vb_s = jnp.stack([s[3] for s in st_s])
    ki_s = jnp.stack([s[4] for s in st_s])
    return (y_p, y_s, ka_p, va_p, kb_p, vb_p, ki_p, ka_s, va_s, kb_s, vb_s, ki_s)
```

```python
import functools
import math

import numpy as np
import jax
import jax.numpy as jnp
from jax import lax
from jax.experimental import pallas as pl
from jax.experimental.pallas import tpu as pltpu

F32 = jnp.float32
BF16 = jnp.bfloat16
I32 = jnp.int32
HIGHEST = lax.Precision.HIGHEST

HEAD_DIM = 128
PAGE_SIZE = 128
MOBA_BLOCK = 256
MOBA_TOPK = 3
DSA_TOPK = 256
IDX_DIM = 64
N_BUCKETS = 32
REL_MAX_DIST = 4096
PEER_NKEYS = 128
PEER_TOPK = 16
RMS_EPS = 1e-6
NEG = -1e30
LANES = 128
ATT_TILE = 256
VMEM_CAP = 56 * 1024 * 1024


def _bucket_thresholds():
    d = np.arange(0, 4 * REL_MAX_DIST + 2)
    max_exact = N_BUCKETS // 2
    nf = np.maximum(d, 1).astype(np.float32)
    large = max_exact + (np.log(nf / np.float32(max_exact)) / np.float32(math.log(REL_MAX_DIST / max_exact))
                         * np.float32(N_BUCKETS - max_exact)).astype(np.int32)
    large = np.minimum(large, N_BUCKETS - 1)
    bucket = np.where(d < max_exact, d, large)
    assert np.all(np.diff(bucket) >= 0) and bucket[-1] == N_BUCKETS - 1
    return [int(np.argmax(bucket >= b)) for b in range(1, N_BUCKETS)]


BUCKET_THR = _bucket_thresholds()


def _cparams(sem, vmem_bytes=None):
    kw = dict(dimension_semantics=sem)
    if vmem_bytes is not None:
        kw["vmem_limit_bytes"] = int(min(max(vmem_bytes, 16 * 1024 * 1024), VMEM_CAP))
    return pltpu.CompilerParams(**kw)


def _bias_from_dist(dist, bias_at):
    val = jnp.full(dist.shape, bias_at(0), F32)
    for b in range(1, N_BUCKETS):
        val = jnp.where(dist >= BUCKET_THR[b - 1], bias_at(b), val)
    return val


def _dot_nt(a, b, precision=None):
    return lax.dot_general(a, b, (((1,), (1,)), ((), ())), precision=precision, preferred_element_type=F32)


def _ada_kernel(c_ref, w_ref, b_ref, o_ref):
    c = c_ref[...]
    s = c * jax.nn.sigmoid(c)
    o_ref[...] = jnp.dot(s, w_ref[...], precision=HIGHEST, preferred_element_type=F32) + b_ref[...]


def _ada_mod(c, w_ada, b_ada):
    n, d = c.shape
    n6 = w_ada.shape[1]
    tn = 512
    return pl.pallas_call(
        _ada_kernel,
        grid=(n6 // tn,),
        in_specs=[pl.BlockSpec((n, d), lambda j: (0, 0)),
                  pl.BlockSpec((d, tn), lambda j: (0, j)),
                  pl.BlockSpec((1, tn), lambda j: (0, j))],
        out_specs=pl.BlockSpec((n, tn), lambda j: (0, j)),
        out_shape=jax.ShapeDtypeStruct((n, n6), F32),
        compiler_params=_cparams(("arbitrary",), 4 * d * tn * 4),
    )(c, w_ada, b_ada.reshape(1, n6))


def _prenorm_kernel(x_ref, g_ref, sc_ref, sh_ref, o_ref):
    x = x_ref[...]
    y = x * lax.rsqrt(jnp.mean(x * x, axis=-1, keepdims=True) + RMS_EPS) * g_ref[...]
    o_ref[...] = (y * (1.0 + sc_ref[0]) + sh_ref[0]).astype(o_ref.dtype)


def _mod_spec(mod, tm, tiles_per_group):
    g, r, d = mod.shape
    if r == 1:
        return pl.BlockSpec((1, 1, d), lambda i, *_: (i // tiles_per_group, 0, 0))
    return pl.BlockSpec((1, tm, d), lambda i, *_: (0, i, 0))


def _prenorm(x, gain, sc, sh, tm, tiles_per_group):
    r, d = x.shape
    return pl.pallas_call(
        _prenorm_kernel,
        grid=(r // tm,),
        in_specs=[pl.BlockSpec((tm, d), lambda i: (i, 0)),
                  pl.BlockSpec((1, d), lambda i: (0, 0)),
                  _mod_spec(sc, tm, tiles_per_group), _mod_spec(sh, tm, tiles_per_group)],
        out_specs=pl.BlockSpec((tm, d), lambda i: (i, 0)),
        out_shape=jax.ShapeDtypeStruct((r, d), BF16),
        compiler_params=_cparams(("parallel",), 8 * tm * d * 4),
    )(x, gain.reshape(1, d), sc, sh)


def _linear_kernel(x_ref, w_ref, *refs, headnorm):
    acc = jnp.dot(x_ref[...], w_ref[...], preferred_element_type=F32)
    if headnorm:
        g_ref, o_ref = refs
        for h in range(acc.shape[1] // HEAD_DIM):
            blk = acc[:, h * HEAD_DIM:(h + 1) * HEAD_DIM]
            ms = jnp.mean(blk * blk, axis=-1, keepdims=True)
            o_ref[:, h * HEAD_DIM:(h + 1) * HEAD_DIM] = blk * lax.rsqrt(ms + RMS_EPS) * g_ref[...]
    else:
        (o_ref,) = refs
        o_ref[...] = acc


def _linear(x, w, tm, head_gain=None):
    r, k = x.shape
    n = w.shape[1]
    in_specs = [pl.BlockSpec((tm, k), lambda i: (i, 0)), pl.BlockSpec((k, n), lambda i: (0, 0))]
    args = [x, w]
    if head_gain is not None:
        in_specs.append(pl.BlockSpec((1, HEAD_DIM), lambda i: (0, 0)))
        args.append(head_gain.reshape(1, HEAD_DIM))
    return pl.pallas_call(
        functools.partial(_linear_kernel, headnorm=head_gain is not None),
        grid=(r // tm,),
        in_specs=in_specs,
        out_specs=pl.BlockSpec((tm, n), lambda i: (i, 0)),
        out_shape=jax.ShapeDtypeStruct((r, n), F32),
        compiler_params=_cparams(("parallel",), 2 * (tm * k * 2 + k * n * 2 + tm * n * 4) + 4 * tm * n * 4),
    )(*args)


def _linear_t_kernel(w_ref, x_ref, o_ref):
    o_ref[...] = _dot_nt(w_ref[...], x_ref[...])


def _linear_t(w_t, x, tm):
    r, k = x.shape
    n = w_t.shape[0]
    return pl.pallas_call(
        _linear_t_kernel,
        grid=(r // tm,),
        in_specs=[pl.BlockSpec((n, k), lambda i: (0, 0)), pl.BlockSpec((tm, k), lambda i: (i, 0))],
        out_specs=pl.BlockSpec((n, tm), lambda i: (0, i)),
        out_shape=jax.ShapeDtypeStruct((n, r), F32),
        compiler_params=_cparams(("parallel",)),
    )(w_t, x)


def _bias_table_kernel(bias_ref, o_ref):
    h = pl.program_id(0)
    shape = o_ref.shape[1:]
    dist = lax.broadcasted_iota(I32, shape, 1) - lax.broadcasted_iota(I32, shape, 0)
    o_ref[0] = _bias_from_dist(dist, lambda b: bias_ref[h, b])


def _bias_tables(rel_bias_t, seq):
    nh = rel_bias_t.shape[0]
    return pl.pallas_call(
        _bias_table_kernel,
        grid=(nh,),
        in_specs=[pl.BlockSpec(memory_space=pltpu.SMEM)],
        out_specs=pl.BlockSpec((1, ATT_TILE, seq), lambda h: (h, 0, 0)),
        out_shape=jax.ShapeDtypeStruct((nh, ATT_TILE, seq), F32),
        compiler_params=_cparams(("parallel",), 4 * ATT_TILE * seq * 4),
    )(rel_bias_t)


def _flash_step(s, m, l, acc, vt_blk):
    m_new = jnp.maximum(m, jnp.max(s, axis=0, keepdims=True))
    alpha = jnp.exp(m - m_new)
    p = jnp.exp(s - m_new)
    l = alpha * l + jnp.sum(p, axis=0, keepdims=True)
    acc = acc * alpha + jnp.dot(vt_blk, p.astype(BF16), preferred_element_type=F32)
    return m_new, l, acc


def _top_rows(s, k):
    n = s.shape[0]
    rows = lax.broadcasted_iota(I32, s.shape, 0)
    vals, picks = [], []
    for _ in range(k):
        m = jnp.max(s, axis=0, keepdims=True)
        first = jnp.min(jnp.where(s == m, rows, n), axis=0, keepdims=True)
        pick = rows == first
        s = jnp.where(pick, -jnp.inf, s)
        vals.append(m)
        picks.append(pick)
    return vals, picks


def _load_kv_scratch(k_ref, v_ref, kbf_ref, vt_ref):
    n_blk = k_ref.shape[0] // ATT_TILE
    for b in range(n_blk):
        sl = slice(b * ATT_TILE, (b + 1) * ATT_TILE)
        kbf_ref[sl, :] = k_ref[sl, :].astype(BF16)
        vt_ref[:, sl] = v_ref[sl, :].T.astype(BF16)


def _moba_kernel(q_ref, k_ref, v_ref, tb_ref, o_ref, kbf_ref, vt_ref, kmean_ref, selb_ref):
    qt = pl.program_id(2)
    n_blk = k_ref.shape[0] // ATT_TILE

    @pl.when(qt == 0)
    def _():
        _load_kv_scratch(k_ref, v_ref, kbf_ref, vt_ref)
        for b in range(n_blk):
            kmean_ref[b:b + 1, :] = jnp.mean(k_ref[b * ATT_TILE:(b + 1) * ATT_TILE, :], axis=0, keepdims=True)

    q = q_ref[...]
    gate = _dot_nt(kmean_ref[...], q, precision=HIGHEST)
    past = lax.broadcasted_iota(I32, gate.shape, 0) < qt
    _, picks = _top_rows(jnp.where(past, gate, NEG), min(MOBA_TOPK, n_blk))
    sel = picks[0]
    for p in picks[1:]:
        sel = sel | p
    selb_ref[...] = jnp.where(sel & past, 0.0, NEG)

    qs = (q * HEAD_DIM ** -0.5).astype(BF16)
    rows = lax.broadcasted_iota(I32, (ATT_TILE, ATT_TILE), 0)
    cols = lax.broadcasted_iota(I32, (ATT_TILE, ATT_TILE), 1)
    k0 = pl.multiple_of(qt * ATT_TILE, ATT_TILE)
    s = _dot_nt(kbf_ref[pl.ds(k0, ATT_TILE), :], qs) + tb_ref[0, :, 0:ATT_TILE]
    s = jnp.where(rows <= cols, s, NEG)
    m0 = jnp.full((1, ATT_TILE), NEG, F32)
    l0 = jnp.zeros((1, ATT_TILE), F32)
    a0 = jnp.zeros((HEAD_DIM, ATT_TILE), F32)
    carry = _flash_step(s, m0, l0, a0, vt_ref[:, pl.ds(k0, ATT_TILE)])

    def body(kb, carry):
        kk = pl.multiple_of(kb * ATT_TILE, ATT_TILE)
        delta = pl.multiple_of((qt - kb) * ATT_TILE, ATT_TILE)
        s = (_dot_nt(kbf_ref[pl.ds(kk, ATT_TILE), :], qs) + tb_ref[0, :, pl.ds(delta, ATT_TILE)]
             + selb_ref[pl.ds(kb, 1), :])
        return _flash_step(s, *carry, vt_ref[:, pl.ds(kk, ATT_TILE)])

    m, l, acc = lax.fori_loop(0, qt, body, carry)
    o_ref[...] = (acc / l).T


def _moba_prompt(q, k, v, tables, batch, seq, n_heads):
    nq = seq // ATT_TILE
    n_blk = seq // MOBA_BLOCK
    return pl.pallas_call(
        _moba_kernel,
        grid=(batch, n_heads, nq),
        in_specs=[pl.BlockSpec((ATT_TILE, HEAD_DIM), lambda b, h, t: (b * nq + t, h)),
                  pl.BlockSpec((seq, HEAD_DIM), lambda b, h, t: (b, h)),
                  pl.BlockSpec((seq, HEAD_DIM), lambda b, h, t: (b, h)),
                  pl.BlockSpec((1, ATT_TILE, seq), lambda b, h, t: (h, 0, 0))],
        out_specs=pl.BlockSpec((ATT_TILE, HEAD_DIM), lambda b, h, t: (b * nq + t, h)),
        out_shape=jax.ShapeDtypeStruct(q.shape, F32),
        scratch_shapes=[pltpu.VMEM((seq, HEAD_DIM), BF16), pltpu.VMEM((HEAD_DIM, seq), BF16),
                        pltpu.VMEM((n_blk, HEAD_DIM), F32), pltpu.VMEM((n_blk, ATT_TILE), F32)],
        compiler_params=_cparams(("parallel", "parallel", "arbitrary"),
                                 2 * (2 * seq * HEAD_DIM * 4 + ATT_TILE * seq * 4) + 8 * 1024 * 1024),
    )(q, k, v, tables)


def _sortable(x):
    bits = lax.bitcast_convert_type(x, I32)
    return bits ^ ((bits >> 31) & 0x7FFFFFFF)


def _dsa_index_kernel(qi_ref, wt_ref, ki_ref, o_ref, u_ref, *, n_sel, n_idx_heads):
    qt = pl.program_id(1)
    nq = ki_ref.shape[0] // ATT_TILE
    w = wt_ref[...] * (IDX_DIM ** -0.5 * n_idx_heads ** -0.5)
    qi = qi_ref[...].astype(BF16)
    rows = lax.broadcasted_iota(I32, (ATT_TILE, ATT_TILE), 0)
    cols = lax.broadcasted_iota(I32, (ATT_TILE, ATT_TILE), 1)

    def score_body(kb, _):
        kk = pl.multiple_of(kb * ATT_TILE, ATT_TILE)
        kblk = ki_ref[pl.ds(kk, ATT_TILE), :].astype(BF16)
        sc = jnp.zeros((ATT_TILE, ATT_TILE), F32)
        for h in range(n_idx_heads):
            sh = _dot_nt(kblk, qi[:, h * IDX_DIM:(h + 1) * IDX_DIM])
            sc = sc + jnp.maximum(sh, 0.0) * w[h:h + 1, :]
        sc = jnp.where((kb < qt) | (rows <= cols), sc, NEG)
        u_ref[pl.ds(kk, ATT_TILE), :] = _sortable(sc)
        return 0

    lax.fori_loop(0, qt + 1, score_body, 0)

    def bit_body(i, t):
        cand = t + lax.shift_left(jnp.int32(1), 31 - i)

        def count_body(kb, cnt):
            kk = pl.multiple_of(kb * ATT_TILE, ATT_TILE)
            hit = (u_ref[pl.ds(kk, ATT_TILE), :] >= cand).astype(I32)
            return cnt + jnp.sum(hit, axis=0, keepdims=True)

        cnt = lax.fori_loop(0, qt + 1, count_body, jnp.zeros((1, ATT_TILE), I32))
        return jnp.where(cnt >= n_sel, cand, t)

    thr = lax.fori_loop(0, 32, bit_body, jnp.full((1, ATT_TILE), -2 ** 31, I32))

    def out_body(kb, _):
        kk = pl.multiple_of(kb * ATT_TILE, ATT_TILE)
        keep = (u_ref[pl.ds(kk, ATT_TILE), :] >= thr) & ((kb < qt) | (rows <= cols))
        o_ref[0, pl.ds(kk, ATT_TILE), :] = jnp.where(keep, 0.0, NEG).astype(o_ref.dtype)
        return 0

    def future_body(kb, _):
        kk = pl.multiple_of(kb * ATT_TILE, ATT_TILE)
        o_ref[0, pl.ds(kk, ATT_TILE), :] = jnp.full((ATT_TILE, ATT_TILE), NEG, o_ref.dtype)
        return 0

    lax.fori_loop(0, qt + 1, out_body, 0)
    lax.fori_loop(qt + 1, nq, future_body, 0)


def _dsa_index(qi, wi_t, ki, batch, seq, n_idx_heads):
    nq = seq // ATT_TILE
    n_sel = min(DSA_TOPK, seq // 4)
    return pl.pallas_call(
        functools.partial(_dsa_index_kernel, n_sel=n_sel, n_idx_heads=n_idx_heads),
        grid=(batch, nq),
        in_specs=[pl.BlockSpec((ATT_TILE, qi.shape[1]), lambda b, t: (b * nq + t, 0)),
                  pl.BlockSpec((n_idx_heads, ATT_TILE), lambda b, t: (0, b * nq + t)),
                  pl.BlockSpec((seq, IDX_DIM), lambda b, t: (b, 0))],
        out_specs=pl.BlockSpec((1, seq, ATT_TILE), lambda b, t: (b, 0, t)),
        out_shape=jax.ShapeDtypeStruct((batch, seq, seq), BF16),
        scratch_shapes=[pltpu.VMEM((seq, ATT_TILE), I32)],
        compiler_params=_cparams(("parallel", "arbitrary"), 32 * 1024 * 1024),
    )(qi, wi_t, ki)


def _dsa_attn_kernel(q_ref, k_ref, v_ref, tb_ref, mask_ref, o_ref, kbf_ref, vt_ref):
    qt = pl.program_id(2)

    @pl.when(qt == 0)
    def _():
        _load_kv_scratch(k_ref, v_ref, kbf_ref, vt_ref)

    qs = (q_ref[...] * HEAD_DIM ** -0.5).astype(BF16)

    def body(kb, carry):
        kk = pl.multiple_of(kb * ATT_TILE, ATT_TILE)
        delta = pl.multiple_of((qt - kb) * ATT_TILE, ATT_TILE)
        s = (_dot_nt(kbf_ref[pl.ds(kk, ATT_TILE), :], qs) + tb_ref[0, :, pl.ds(delta, ATT_TILE)]
             + mask_ref[0, pl.ds(kk, ATT_TILE), :].astype(F32))
        return _flash_step(s, *carry, vt_ref[:, pl.ds(kk, ATT_TILE)])

    m0 = jnp.full((1, ATT_TILE), NEG, F32)
    l0 = jnp.zeros((1, ATT_TILE), F32)
    a0 = jnp.zeros((HEAD_DIM, ATT_TILE), F32)
    m, l, acc = lax.fori_loop(0, qt + 1, body, (m0, l0, a0))
    o_ref[...] = (acc / l).T


def _dsa_prompt(q, k, v, tables, mask, batch, seq, n_heads, head_off):
    nq = seq // ATT_TILE
    return pl.pallas_call(
        _dsa_attn_kernel,
        grid=(batch, n_heads, nq),
        in_specs=[pl.BlockSpec((ATT_TILE, HEAD_DIM), lambda b, h, t: (b * nq + t, h)),
                  pl.BlockSpec((seq, HEAD_DIM), lambda b, h, t: (b, h)),
                  pl.BlockSpec((seq, HEAD_DIM), lambda b, h, t: (b, h)),
                  pl.BlockSpec((1, ATT_TILE, seq), lambda b, h, t: (h + head_off, 0, 0)),
                  pl.BlockSpec((1, seq, ATT_TILE), lambda b, h, t: (b, 0, t))],
        out_specs=pl.BlockSpec((ATT_TILE, HEAD_DIM), lambda b, h, t: (b * nq + t, h)),
        out_shape=jax.ShapeDtypeStruct(q.shape, F32),
        scratch_shapes=[pltpu.VMEM((seq, HEAD_DIM), BF16), pltpu.VMEM((HEAD_DIM, seq), BF16)],
        compiler_params=_cparams(("parallel", "parallel", "arbitrary"),
                                 2 * (2 * seq * HEAD_DIM * 4 + ATT_TILE * seq * 6) + 8 * 1024 * 1024),
    )(q, k, v, tables, mask)


def _merge_kernel(oa_ref, ob_ref, ga_ref, gb_ref, wa_ref, wb_ref, o_ref):
    bra = jnp.dot(oa_ref[...].astype(BF16), wa_ref[...], preferred_element_type=F32)
    brb = jnp.dot(ob_ref[...].astype(BF16), wb_ref[...], preferred_element_type=F32)
    o_ref[...] = (jax.nn.sigmoid(ga_ref[...]) * bra + jax.nn.sigmoid(gb_ref[...]) * brb).astype(o_ref.dtype)


def _merge(out_a, out_b, ga, gb, w_pa, w_pb, tm):
    r, w = out_a.shape
    d = ga.shape[1]
    row = lambda n: pl.BlockSpec((tm, n), lambda i: (i, 0))
    full = lambda a: pl.BlockSpec(a.shape, lambda i: (0, 0))
    return pl.pallas_call(
        _merge_kernel,
        grid=(r // tm,),
        in_specs=[row(w), row(w), row(d), row(d), full(w_pa), full(w_pb)],
        out_specs=row(d),
        out_shape=jax.ShapeDtypeStruct((r, d), BF16),
        compiler_params=_cparams(("parallel",), 4 * w * d * 2 + 8 * tm * d * 4),
    )(out_a, out_b, ga, gb, w_pa, w_pb)


def _outproj_kernel(mg_ref, w_ref, x_ref, g1_ref, ng_ref, sc_ref, sh_ref, h_ref, hm_ref):
    h = x_ref[...] + g1_ref[0] * jnp.dot(mg_ref[...], w_ref[...], preferred_element_type=F32)
    h_ref[...] = h
    y = h * lax.rsqrt(jnp.mean(h * h, axis=-1, keepdims=True) + RMS_EPS) * ng_ref[...]
    hm_ref[...] = (y * (1.0 + sc_ref[0]) + sh_ref[0]).astype(hm_ref.dtype)


def _outproj(merged, w_out, x, g1, norm_g, sc2, sh2, tm, tiles_per_group):
    r, d = x.shape
    row = pl.BlockSpec((tm, d), lambda i: (i, 0))
    ms = lambda m: _mod_spec(m, tm, tiles_per_group)
    return pl.pallas_call(
        _outproj_kernel,
        grid=(r // tm,),
        in_specs=[row, pl.BlockSpec((d, d), lambda i: (0, 0)), row, ms(g1),
                  pl.BlockSpec((1, d), lambda i: (0, 0)), ms(sc2), ms(sh2)],
        out_specs=[row, row],
        out_shape=[jax.ShapeDtypeStruct((r, d), F32), jax.ShapeDtypeStruct((r, d), BF16)],
        compiler_params=_cparams(("parallel",), 4 * d * d * 2 + 12 * tm * d * 4),
    )(merged, w_out, x, g1, norm_g.reshape(1, d), sc2, sh2)


def _peer_pairs(k):
    return [(i, j) for i in range(k) for j in range(k) if (i + 1) * (j + 1) <= k]


def _peer_route_kernel(q_ref, k1_ref, k2_ref, e1_ref, e2_ref, tau_ref):
    n_heads = k1_ref.shape[0]
    dk = k1_ref.shape[2]
    kk = PEER_TOPK + 1
    for h in range(n_heads):
        q1 = q_ref[:, (2 * h) * dk:(2 * h + 1) * dk]
        q2 = q_ref[:, (2 * h + 1) * dk:(2 * h + 2) * dk]
        s1 = _dot_nt(k1_ref[h], q1, precision=HIGHEST)
        s2 = _dot_nt(k2_ref[h], q2, precision=HIGHEST)
        v1, _ = _top_rows(s1, kk)
        v2, _ = _top_rows(s2, kk)
        cand = jnp.concatenate([v1[i] + v2[j] for i, j in _peer_pairs(kk)], axis=0)
        c, _ = _top_rows(cand, kk)
        top = c[0]
        z = sum(jnp.exp(ck - top) for ck in c[:PEER_TOPK])
        inv_z = 1.0 / z
        e1_ref[h] = jnp.exp(s1 - v1[0]) * inv_z
        e2_ref[h] = jnp.exp(s2 - v2[0])
        tau_ref[h:h + 1, :] = jnp.exp(0.5 * (c[PEER_TOPK - 1] + c[PEER_TOPK]) - top) * inv_z


def _peer_route(qp, keys1, keys2, tm):
    r, d = qp.shape
    nh, nk, dk = keys1.shape
    return pl.pallas_call(
        _peer_route_kernel,
        grid=(r // tm,),
        in_specs=[pl.BlockSpec((tm, d), lambda i: (i, 0)),
                  pl.BlockSpec((nh, nk, dk), lambda i: (0, 0, 0)),
                  pl.BlockSpec((nh, nk, dk), lambda i: (0, 0, 0))],
        out_specs=[pl.BlockSpec((nh, nk, tm), lambda i: (0, 0, i)),
                   pl.BlockSpec((nh, nk, tm), lambda i: (0, 0, i)),
                   pl.BlockSpec((nh, tm), lambda i: (0, i))],
        out_shape=[jax.ShapeDtypeStruct((nh, nk, r), F32), jax.ShapeDtypeStruct((nh, nk, r), F32),
                   jax.ShapeDtypeStruct((nh, r), F32)],
        compiler_params=_cparams(("parallel",), 32 * 1024 * 1024),
    )(qp, keys1, keys2)


def _gelu_tanh(x):
    return 0.5 * x * (1.0 + jnp.tanh(math.sqrt(2.0 / math.pi) * (x + 0.044715 * (x * x * x))))


def _peer_dense_kernel(hm_ref, u_ref, v_ref, e1_ref, e2_ref, tau_ref, o_ref):
    j = pl.program_id(1)
    te = u_ref.shape[0]
    n_heads = e1_ref.shape[0]
    rows_per_tile = te // PEER_NKEYS

    @pl.when(j == 0)
    def _():
        o_ref[...] = jnp.zeros_like(o_ref)

    act = _gelu_tanh(_dot_nt(u_ref[...], hm_ref[...]))
    blocks = []
    for r in range(rows_per_tile):
        i1 = j * rows_per_tile + r
        g = jnp.zeros((PEER_NKEYS, act.shape[1]), F32)
        for h in range(n_heads):
            w = e2_ref[h] * e1_ref[h, pl.ds(i1, 1), :]
            g = g + jnp.where(w >= tau_ref[h:h + 1, :], w, 0.0)
        blocks.append(g * act[r * PEER_NKEYS:(r + 1) * PEER_NKEYS, :])
    a_t = jnp.concatenate(blocks, axis=0) if rows_per_tile > 1 else blocks[0]
    o_ref[...] += jnp.dot(a_t.T.astype(BF16), v_ref[...], preferred_element_type=F32)


def _peer_dense(hm, u, v, e1n, e2, tau, tm, te):
    r, d = hm.shape
    n_exp = u.shape[0]
    nh, nk, _ = e1n.shape
    return pl.pallas_call(
        _peer_dense_kernel,
        grid=(r // tm, n_exp // te),
        in_specs=[pl.BlockSpec((tm, d), lambda i, j: (i, 0)),
                  pl.BlockSpec((te, d), lambda i, j: (j, 0)),
                  pl.BlockSpec((te, d), lambda i, j: (j, 0)),
                  pl.BlockSpec((nh, nk, tm), lambda i, j: (0, 0, i)),
                  pl.BlockSpec((nh, nk, tm), lambda i, j: (0, 0, i)),
                  pl.BlockSpec((nh, tm), lambda i, j: (0, i))],
        out_specs=pl.BlockSpec((tm, d), lambda i, j: (i, 0)),
        out_shape=jax.ShapeDtypeStruct((r, d), F32),
        compiler_params=_cparams(("parallel", "arbitrary"),
                                 2 * (tm * d * 2 + 2 * te * d * 2 + 2 * nh * nk * tm * 4 + tm * d * 4)
                                 + 8 * te * tm * 4),
    )(hm, u, v, e1n, e2, tau)


def _residual_kernel(h_ref, p_ref, g2_ref, o_ref):
    o_ref[...] = h_ref[...] + g2_ref[0] * p_ref[...]


def _residual(h, p, g2, tm, tiles_per_group):
    r, d = h.shape
    row = pl.BlockSpec((tm, d), lambda i: (i, 0))
    return pl.pallas_call(
        _residual_kernel,
        grid=(r // tm,),
        in_specs=[row, row, _mod_spec(g2, tm, tiles_per_group)],
        out_specs=row,
        out_shape=jax.ShapeDtypeStruct((r, d), F32),
        compiler_params=_cparams(("parallel",), 8 * tm * d * 4),
    )(h, p, g2)


def _sample_scan_kernel(pt_ref, ka_ref, kidx_ref, qa_ref, qi_ref, wi_ref, kinew_ref,
                        blk_ref, score_ref, ksum_ref, *, n_idx_heads, n_heads):
    p = pl.program_id(1)
    n_pages = pl.num_programs(1)
    pages_per_blk = MOBA_BLOCK // PAGE_SIZE
    n_blk = ksum_ref.shape[0]

    @pl.when(p == 0)
    def _():
        ksum_ref[...] = jnp.zeros_like(ksum_ref)
        score_ref[0, pl.ds(n_pages, 8), :] = jnp.full((8, PAGE_SIZE), -jnp.inf, F32)

    b = p // pages_per_blk
    ksum_ref[b] += jnp.sum(ka_ref[0, 0], axis=0)

    w = wi_ref[0] * (IDX_DIM ** -0.5 * n_idx_heads ** -0.5)
    qi = qi_ref[0]
    sc = jnp.maximum(_dot_nt(qi.astype(BF16), kidx_ref[0, 0].astype(BF16)), 0.0) * w
    score_ref[0, pl.ds(p, 1), :] = jnp.sum(sc, axis=0, keepdims=True)

    @pl.when(p == n_pages - 1)
    def _():
        kn = kinew_ref[0].astype(BF16).astype(F32)
        dots = jnp.sum(qi.astype(BF16).astype(F32) * kn, axis=-1, keepdims=True)
        new_score = jnp.sum(jnp.maximum(dots, 0.0) * w, axis=0, keepdims=True)
        lane = lax.broadcasted_iota(I32, (1, PAGE_SIZE), 1)
        score_ref[0, pl.ds(n_pages, 1), :] = jnp.where(lane == 0, new_score, -jnp.inf)

        gate = jnp.sum(ksum_ref[...] * (qa_ref[...] * (1.0 / MOBA_BLOCK)), axis=-1)
        rows = lax.broadcasted_iota(I32, gate.shape, 0)
        out = jnp.zeros(blk_ref.shape[1:], I32)
        orow = lax.broadcasted_iota(I32, out.shape, 0)
        for t in range(MOBA_TOPK):
            m = jnp.max(gate, axis=0, keepdims=True)
            first = jnp.min(jnp.where(gate == m, rows, n_blk), axis=0, keepdims=True)
            gate = jnp.where(rows == first, -jnp.inf, gate)
            padded = jnp.concatenate([first, jnp.zeros((1, out.shape[1] - n_heads), I32)], axis=1)
            out = jnp.where(orow == t, padded, out)
        blk_ref[0] = out


def _sample_scan(page_table, cache_ka, cache_kidx, qa, qi, wi, ki_new, n_heads, n_idx_heads):
    ns, n_pages = page_table.shape
    n_blk = n_pages * PAGE_SIZE // MOBA_BLOCK
    grid_spec = pltpu.PrefetchScalarGridSpec(
        num_scalar_prefetch=1,
        grid=(ns, n_pages),
        in_specs=[pl.BlockSpec((1, 1, PAGE_SIZE, n_heads, HEAD_DIM), lambda s, p, pt: (0, pt[s, p], 0, 0, 0)),
                  pl.BlockSpec((1, 1, PAGE_SIZE, IDX_DIM), lambda s, p, pt: (0, pt[s, p], 0, 0)),
                  pl.BlockSpec((1, n_heads, HEAD_DIM), lambda s, p, pt: (s, 0, 0)),
                  pl.BlockSpec((1, n_idx_heads, IDX_DIM), lambda s, p, pt: (s, 0, 0)),
                  pl.BlockSpec((1, n_idx_heads, 1), lambda s, p, pt: (s, 0, 0)),
                  pl.BlockSpec((1, 1, IDX_DIM), lambda s, p, pt: (s, 0, 0))],
        out_specs=[pl.BlockSpec((1, 8, LANES), lambda s, p, pt: (s, 0, 0)),
                   pl.BlockSpec((1, n_pages + 8, PAGE_SIZE), lambda s, p, pt: (s, 0, 0))],
        scratch_shapes=[pltpu.VMEM((n_blk, n_heads, HEAD_DIM), F32)])
    return pl.pallas_call(
        functools.partial(_sample_scan_kernel, n_idx_heads=n_idx_heads, n_heads=n_heads),
        grid_spec=grid_spec,
        out_shape=[jax.ShapeDtypeStruct((ns, 8, LANES), I32),
                   jax.ShapeDtypeStruct((ns, n_pages + 8, PAGE_SIZE), F32)],
        compiler_params=_cparams(("parallel", "arbitrary")),
    )(page_table, cache_ka, cache_kidx, qa, qi, wi, ki_new)


def _sample_topk_kernel(score_ref, idx_ref, *, n_sel, n_valid):
    s = score_ref[...]
    n = s.shape[1]
    lane = lax.broadcasted_iota(I32, s.shape, 1)
    s = jnp.where(lane < n_valid, s, -jnp.inf)
    out_lane = lax.broadcasted_iota(I32, idx_ref.shape, 1)

    def body(t, carry):
        s, out = carry
        m = jnp.max(s, axis=1, keepdims=True)
        first = jnp.min(jnp.where(s == m, lane, n), axis=1, keepdims=True)
        s = jnp.where(lane == first, -jnp.inf, s)
        return s, jnp.where(out_lane == t, first, out)

    _, out = lax.fori_loop(0, n_sel, body, (s, jnp.zeros(idx_ref.shape, I32)))
    idx_ref[...] = out


def _sample_topk(score, n_sel, n_valid):
    ns, n = score.shape
    return pl.pallas_call(
        functools.partial(_sample_topk_kernel, n_sel=n_sel, n_valid=n_valid),
        out_shape=jax.ShapeDtypeStruct((ns, n_sel), I32),
        compiler_params=_cparams(None, 8 * ns * n * 4),
    )(score)


def _sample_moba_kernel(pt_ref, blk_ref, bias_ref, q_ref, k_ref, v_ref, kn_ref, vn_ref, o_ref,
                        m_ref, l_ref, acc_ref, *, past):
    s_id, h, j = pl.program_id(0), pl.program_id(1), pl.program_id(2)
    n_j = pl.num_programs(2)
    pages_per_blk = MOBA_BLOCK // PAGE_SIZE
    q8 = jnp.broadcast_to(q_ref[0, pl.ds(h, 1), :] * HEAD_DIM ** -0.5, (8, HEAD_DIM))

    @pl.when(j == 0)
    def _():
        kn = kn_ref[0, pl.ds(h, 1), :].astype(BF16).astype(F32)
        s0 = jnp.sum(q8.astype(BF16).astype(F32) * kn, axis=-1, keepdims=True)[0:1] + bias_ref[h, 0]
        m_ref[...] = jnp.broadcast_to(s0, m_ref.shape)
        l_ref[...] = jnp.ones_like(l_ref)
        acc_ref[...] = vn_ref[0, pl.ds(h, 1), :].astype(BF16).astype(F32)

    blk = blk_ref[s_id, (h * MOBA_TOPK + j // pages_per_blk)]
    key0 = blk * MOBA_BLOCK + (j % pages_per_blk) * PAGE_SIZE
    dist = past - (key0 + lax.broadcasted_iota(I32, (1, PAGE_SIZE), 1))
    k_page = k_ref[0, 0, :, h, :].astype(BF16)
    v_page = v_ref[0, 0, :, h, :].astype(BF16)
    s = _dot_nt(q8.astype(BF16), k_page)[0:1] + _bias_from_dist(dist, lambda b: bias_ref[h, b])
    s = jnp.where(dist >= 0, s, NEG)
    m_old = m_ref[...]
    m_new = jnp.maximum(m_old, jnp.max(s, axis=-1, keepdims=True))
    alpha = jnp.exp(m_old - m_new)
    p = jnp.exp(s - m_new)
    l_ref[...] = alpha * l_ref[...] + jnp.sum(p, axis=-1, keepdims=True)
    p8 = jnp.broadcast_to(p, (8, PAGE_SIZE)).astype(BF16)
    acc_ref[...] = alpha * acc_ref[...] + jnp.dot(p8, v_page, preferred_element_type=F32)[0:1]
    m_ref[...] = m_new

    @pl.when(j == n_j - 1)
    def _():
        o_ref[0, pl.ds(h, 1), :] = acc_ref[...] / l_ref[...]


def _sample_moba(page_table, blk_flat, rel_bias_t, qa, cache_ka, cache_va, ka_new, va_new, n_heads):
    ns, n_pages = page_table.shape
    pages_per_blk = MOBA_BLOCK // PAGE_SIZE
    n_j = MOBA_TOPK * pages_per_blk

    def page_map(s, h, j, pt, blk):
        page = pt[s, blk[s, h * MOBA_TOPK + j // pages_per_blk] * pages_per_blk + j % pages_per_blk]
        return (0, page, 0, 0, 0)

    heads = pl.BlockSpec((1, n_heads, HEAD_DIM), lambda s, h, j, pt, blk: (s, 0, 0))
    page = pl.BlockSpec((1, 1, PAGE_SIZE, n_heads, HEAD_DIM), page_map)
    grid_spec = pltpu.PrefetchScalarGridSpec(
        num_scalar_prefetch=2,
        grid=(ns, n_heads, n_j),
        in_specs=[pl.BlockSpec(memory_space=pltpu.SMEM), heads, page, page, heads, heads],
        out_specs=heads,
        scratch_shapes=[pltpu.VMEM((1, 1), F32), pltpu.VMEM((1, 1), F32), pltpu.VMEM((1, HEAD_DIM), F32)])
    return pl.pallas_call(
        functools.partial(_sample_moba_kernel, past=n_pages * PAGE_SIZE),
        grid_spec=grid_spec,
        out_shape=jax.ShapeDtypeStruct(qa.shape, F32),
        compiler_params=_cparams(("parallel", "arbitrary", "arbitrary")),
    )(page_table, blk_flat, rel_bias_t, qa, cache_ka, cache_va, ka_new, va_new)


def _sample_dsa_kernel(pt_ref, sel_ref, bias_ref, selv_ref, q_ref, kn_ref, vn_ref, ck_ref, cv_ref, o_ref,
                       kbuf, vbuf, sem, *, past, n_heads, head_off):
    s_id = pl.program_id(0)
    n_sel = kbuf.shape[0]

    def row_copies(r):
        idx = sel_ref[s_id, r]
        page = pt_ref[s_id, jnp.minimum(idx, past - 1) // PAGE_SIZE]
        pos = idx % PAGE_SIZE
        dst_k, dst_v = kbuf.at[pl.ds(r, 1)], vbuf.at[pl.ds(r, 1)]
        cached = (pltpu.make_async_copy(ck_ref.at[0, page, pl.ds(pos, 1)], dst_k, sem.at[0]),
                  pltpu.make_async_copy(cv_ref.at[0, page, pl.ds(pos, 1)], dst_v, sem.at[1]))
        fresh = (pltpu.make_async_copy(kn_ref.at[pl.ds(s_id, 1)], dst_k, sem.at[0]),
                 pltpu.make_async_copy(vn_ref.at[pl.ds(s_id, 1)], dst_v, sem.at[1]))
        return idx, cached, fresh

    def start(r, _):
        idx, cached, fresh = row_copies(r)

        @pl.when(idx < past)
        def _():
            for c in cached:
                c.start()

        @pl.when(idx >= past)
        def _():
            for c in fresh:
                c.start()
        return 0

    def wait(r, _):
        idx, cached, fresh = row_copies(r)

        @pl.when(idx < past)
        def _():
            for c in cached:
                c.wait()

        @pl.when(idx >= past)
        def _():
            for c in fresh:
                c.wait()
        return 0

    lax.fori_loop(0, n_sel, start, 0)
    lax.fori_loop(0, n_sel, wait, 0)

    dist = past - selv_ref[0]
    outs = []
    for h in range(n_heads):
        q8 = jnp.broadcast_to(q_ref[0, h:h + 1, :] * HEAD_DIM ** -0.5, (8, HEAD_DIM)).astype(BF16)
        s = _dot_nt(q8, kbuf[:, h, :].astype(BF16))[0:1]
        s = s + _bias_from_dist(dist, lambda b: bias_ref[h + head_off, b])
        s = jnp.where(dist >= 0, s, NEG)
        p = jnp.exp(s - jnp.max(s, axis=-1, keepdims=True))
        l = jnp.sum(p, axis=-1, keepdims=True)
        p8 = jnp.broadcast_to(p, (8, n_sel)).astype(BF16)
        outs.append(jnp.dot(p8, vbuf[:, h, :].astype(BF16), preferred_element_type=F32)[0:1] / l)
    o_ref[0] = jnp.concatenate(outs, axis=0)


def _sample_dsa(page_table, sel, rel_bias_t, qb, kb_new, vb_new, cache_kb, cache_vb, n_heads, head_off):
    ns, n_pages = page_table.shape
    n_sel = sel.shape[1]
    row = pl.BlockSpec((1, n_heads, HEAD_DIM), lambda s, pt, sl: (s, 0, 0))
    hbm = pl.BlockSpec(memory_space=pl.ANY)
    grid_spec = pltpu.PrefetchScalarGridSpec(
        num_scalar_prefetch=2,
        grid=(ns,),
        in_specs=[pl.BlockSpec(memory_space=pltpu.SMEM),
                  pl.BlockSpec((1, 1, n_sel), lambda s, pt, sl: (s, 0, 0)),
                  row, hbm, hbm, hbm, hbm],
        out_specs=row,
        scratch_shapes=[pltpu.VMEM((n_sel, n_heads, HEAD_DIM), F32), pltpu.VMEM((n_sel, n_heads, HEAD_DIM), F32),
                        pltpu.SemaphoreType.DMA((2,))])
    return pl.pallas_call(
        functools.partial(_sample_dsa_kernel, past=n_pages * PAGE_SIZE, n_heads=n_heads, head_off=head_off),
        grid_spec=grid_spec,
        out_shape=jax.ShapeDtypeStruct(qb.shape, F32),
        compiler_params=_cparams(("arbitrary",)),
    )(page_table, sel, rel_bias_t, sel.reshape(ns, 1, n_sel), qb, kb_new, vb_new, cache_kb, cache_vb)


def _split_weights(w_in, w_a, w_b, n_idx_heads, d):
    sizes = (w_a, w_a, w_a, w_b, w_b, w_b, n_idx_heads * IDX_DIM, IDX_DIM, n_idx_heads, d, d)
    offs = np.concatenate([[0], np.cumsum(sizes)])
    return [w_in[:, int(offs[i]):int(offs[i + 1])].astype(BF16) for i in range(len(sizes))]


def _layer_rows(x, mods, tm, tiles_per_group, lw, attend):
    (norm_mix_g, w_sec, qn_a, kn_a, qn_b, kn_b, w_pa, w_pb, w_out, norm_ffn_g, w_qp, keys1, keys2, pu, pv) = lw
    sh1, sc1, g1, sh2, sc2, g2 = mods
    r, d = x.shape
    xm = _prenorm(x, norm_mix_g, sc1, sh1, tm, tiles_per_group)
    w_qa, w_ka, w_va, w_qb, w_kb, w_vb, w_qi, w_ki, w_wi, w_ga, w_gb = w_sec
    qa = _linear(xm, w_qa, tm, qn_a)
    ka = _linear(xm, w_ka, tm, kn_a)
    va = _linear(xm, w_va, tm)
    qb = _linear(xm, w_qb, tm, qn_b)
    kb = _linear(xm, w_kb, tm, kn_b)
    vb = _linear(xm, w_vb, tm)
    qi = _linear(xm, w_qi, tm)
    ki = _linear(xm, w_ki, tm)
    wi_t = _linear_t(w_wi.T, xm, tm)
    ga = _linear(xm, w_ga, tm)
    gb = _linear(xm, w_gb, tm)
    out_a, out_b = attend(qa, ka, va, qb, kb, vb, qi, ki, wi_t)
    merged = _merge(out_a, out_b, ga, gb, w_pa, w_pb, tm)
    h, hm = _outproj(merged, w_out, x, g1, norm_ffn_g, sc2, sh2, tm, tiles_per_group)
    qp = _linear(hm, w_qp, tm)
    tm_peer = min(r, 512)
    e1n, e2, tau = _peer_route(qp, keys1, keys2, min(r, 256))
    peer = _peer_dense(hm, pu, pv, e1n, e2, tau, tm_peer, 512)
    y = _residual(h, peer, g2, tm, tiles_per_group)
    return y, (ka, va, kb, vb, ki)


def kernel(x_prompt, x_sample, cache_k_a, cache_v_a, cache_k_b, cache_v_b, cache_k_idx, page_table, c_prompt, c_sample, rel_bias, w_ada, b_ada, norm_mix_g, w_in, qnorm_a, knorm_a, qnorm_b, knorm_b, w_proj_a, w_proj_b, w_out, norm_ffn_g, w_q_peer, peer_keys1, peer_keys2, peer_u, peer_v):
    batch, seq, d = x_prompt.shape
    ns, dec_seq, _ = x_sample.shape
    depth = w_ada.shape[0]
    assert depth == 1 and dec_seq == 1 and seq % ATT_TILE == 0
    n_heads = d // 256
    n_idx_heads = d // 128
    w_att = n_heads * HEAD_DIM
    n_pages = page_table.shape[1]
    past = n_pages * PAGE_SIZE
    ns_pad = -(-ns // LANES) * LANES

    n_c = batch + ns_pad
    n_c_pad = -(-n_c // 8) * 8
    c_all = jnp.zeros((n_c_pad, d), F32).at[:batch].set(c_prompt).at[batch:batch + ns].set(c_sample)
    mod = _ada_mod(c_all, w_ada[0], b_ada[0])
    mods_p = tuple(mod[:batch, i * d:(i + 1) * d].reshape(batch, 1, d) for i in range(6))
    mods_s = tuple(mod[batch:batch + ns_pad, i * d:(i + 1) * d].reshape(1, ns_pad, d) for i in range(6))

    lw = (norm_mix_g[0], _split_weights(w_in[0], w_att, w_att, n_idx_heads, d),
          qnorm_a[0], knorm_a[0], qnorm_b[0], knorm_b[0],
          w_proj_a[0].astype(BF16), w_proj_b[0].astype(BF16), w_out[0].astype(BF16), norm_ffn_g[0],
          w_q_peer[0].astype(BF16), peer_keys1[0], peer_keys2[0],
          peer_u[0].astype(BF16), peer_v[0].astype(BF16))
    rel_bias_t = rel_bias.T
    tables = _bias_tables(rel_bias_t, seq)

    def attend_prompt(qa, ka, va, qb, kb, vb, qi, ki, wi_t):
        out_a = _moba_prompt(qa, ka, va, tables, batch, seq, n_heads)
        mask = _dsa_index(qi, wi_t, ki, batch, seq, n_idx_heads)
        out_b = _dsa_prompt(qb, kb, vb, tables, mask, batch, seq, n_heads, n_heads)
        return out_a, out_b

    tm_p = 512 if seq % 512 == 0 else ATT_TILE
    y_p, new_p = _layer_rows(x_prompt.reshape(batch * seq, d), mods_p, tm_p, seq // tm_p, lw, attend_prompt)

    def attend_sample(qa, ka, va, qb, kb, vb, qi, ki, wi_t):
        hd = lambda a: a[:ns].reshape(ns, n_heads, HEAD_DIM)
        wi = wi_t[:, :ns].T.reshape(ns, n_idx_heads, 1)
        blk, score = _sample_scan(page_table, cache_k_a, cache_k_idx, hd(qa),
                                  qi[:ns].reshape(ns, n_idx_heads, IDX_DIM), wi,
                                  ki[:ns].reshape(ns, 1, IDX_DIM), n_heads, n_idx_heads)
        blk_flat = blk[:, :MOBA_TOPK, :n_heads].transpose(0, 2, 1).reshape(ns, n_heads * MOBA_TOPK)
        out_a = _sample_moba(page_table, blk_flat, rel_bias_t, hd(qa), cache_k_a, cache_v_a, hd(ka), hd(va),
                             n_heads)
        n_sel = min(DSA_TOPK, (past + 1) // 4)
        sel = _sample_topk(score.reshape(ns, -1), n_sel, past + 1)
        out_b = _sample_dsa(page_table, sel, rel_bias_t, hd(qb), hd(kb), hd(vb), cache_k_b, cache_v_b,
                            n_heads, n_heads)
        pad = lambda a: jnp.zeros((ns_pad, w_att), F32).at[:ns].set(a.reshape(ns, w_att))
        return pad(out_a), pad(out_b)

    x_s = jnp.zeros((ns_pad, d), F32).at[:ns].set(x_sample.reshape(ns, d))
    y_s, new_s = _layer_rows(x_s, mods_s, ns_pad, 1, lw, attend_sample)

    def state(new, n_seq, t):
        ka, va, kb, vb, ki = new
        hd = lambda a: a[:n_seq * t].reshape(1, n_seq, t, n_heads, HEAD_DIM)
        return hd(ka), hd(va), hd(kb), hd(vb), ki[:n_seq * t].reshape(1, n_seq, t, IDX_DIM)

    return ((y_p.reshape(batch, seq, d), y_s[:ns].reshape(ns, 1, d))
            + state(new_p, batch, seq) + state(new_s, ns, 1))
```

```python
import functools
import math

import numpy as np
import jax
import jax.numpy as jnp
from jax import lax
from jax.experimental import pallas as pl
from jax.experimental.pallas import tpu as pltpu

F32 = jnp.float32
BF16 = jnp.bfloat16
I32 = jnp.int32
HIGHEST = lax.Precision.HIGHEST

HEAD_DIM = 128
PAGE_SIZE = 128
MOBA_BLOCK = 256
MOBA_TOPK = 3
DSA_TOPK = 256
IDX_DIM = 64
N_BUCKETS = 32
REL_MAX_DIST = 4096
PEER_NKEYS = 128
PEER_TOPK = 16
RMS_EPS = 1e-6
NEG = -1e30
LANES = 128
ATT_TILE = 256
VMEM_CAP = 56 * 1024 * 1024
HEADS_PER_STEP = 2
SCAN_BUFS = 8


def _bucket_thresholds():
    d = np.arange(0, 4 * REL_MAX_DIST + 2)
    max_exact = N_BUCKETS // 2
    nf = np.maximum(d, 1).astype(np.float32)
    large = max_exact + (np.log(nf / np.float32(max_exact)) / np.float32(math.log(REL_MAX_DIST / max_exact))
                         * np.float32(N_BUCKETS - max_exact)).astype(np.int32)
    large = np.minimum(large, N_BUCKETS - 1)
    bucket = np.where(d < max_exact, d, large)
    assert np.all(np.diff(bucket) >= 0) and bucket[-1] == N_BUCKETS - 1
    return [int(np.argmax(bucket >= b)) for b in range(1, N_BUCKETS)]


BUCKET_THR = _bucket_thresholds()


def _cparams(sem, vmem_bytes=None):
    kw = dict(dimension_semantics=sem)
    if vmem_bytes is not None:
        kw["vmem_limit_bytes"] = int(min(max(vmem_bytes, 16 * 1024 * 1024), VMEM_CAP))
    return pltpu.CompilerParams(**kw)


def _bias_from_dist(dist, bias_at):
    val = jnp.full(dist.shape, bias_at(0), F32)
    for b in range(1, N_BUCKETS):
        val = jnp.where(dist >= BUCKET_THR[b - 1], bias_at(b), val)
    return val


def _dot_nt(a, b, precision=None):
    return lax.dot_general(a, b, (((1,), (1,)), ((), ())), precision=precision, preferred_element_type=F32)


def _ada_kernel(c_ref, w_ref, b_ref, o_ref):
    c = c_ref[...]
    s = c * jax.nn.sigmoid(c)
    o_ref[...] = jnp.dot(s, w_ref[...], precision=HIGHEST, preferred_element_type=F32) + b_ref[...]


def _ada_mod(c, w_ada, b_ada):
    n, d = c.shape
    n6 = w_ada.shape[1]
    tn = 512
    return pl.pallas_call(
        _ada_kernel,
        grid=(n6 // tn,),
        in_specs=[pl.BlockSpec((n, d), lambda j: (0, 0)),
                  pl.BlockSpec((d, tn), lambda j: (0, j)),
                  pl.BlockSpec((1, tn), lambda j: (0, j))],
        out_specs=pl.BlockSpec((n, tn), lambda j: (0, j)),
        out_shape=jax.ShapeDtypeStruct((n, n6), F32),
        compiler_params=_cparams(("arbitrary",), 4 * d * tn * 4),
    )(c, w_ada, b_ada.reshape(1, n6))


def _prenorm_kernel(x_ref, g_ref, sc_ref, sh_ref, o_ref):
    x = x_ref[...]
    y = x * lax.rsqrt(jnp.mean(x * x, axis=-1, keepdims=True) + RMS_EPS) * g_ref[...]
    o_ref[...] = (y * (1.0 + sc_ref[0]) + sh_ref[0]).astype(o_ref.dtype)


def _mod_spec(mod, tm, tiles_per_group):
    g, r, d = mod.shape
    if r == 1:
        return pl.BlockSpec((1, 1, d), lambda i, *_: (i // tiles_per_group, 0, 0))
    return pl.BlockSpec((1, tm, d), lambda i, *_: (0, i, 0))


def _prenorm(x, gain, sc, sh, tm, tiles_per_group):
    r, d = x.shape
    return pl.pallas_call(
        _prenorm_kernel,
        grid=(r // tm,),
        in_specs=[pl.BlockSpec((tm, d), lambda i: (i, 0)),
                  pl.BlockSpec((1, d), lambda i: (0, 0)),
                  _mod_spec(sc, tm, tiles_per_group), _mod_spec(sh, tm, tiles_per_group)],
        out_specs=pl.BlockSpec((tm, d), lambda i: (i, 0)),
        out_shape=jax.ShapeDtypeStruct((r, d), BF16),
        compiler_params=_cparams(("parallel",), 8 * tm * d * 4),
    )(x, gain.reshape(1, d), sc, sh)


def _linear_kernel(x_ref, w_ref, *refs, headnorm):
    acc = jnp.dot(x_ref[...], w_ref[...], preferred_element_type=F32)
    if headnorm:
        g_ref, o_ref = refs
        for h in range(acc.shape[1] // HEAD_DIM):
            blk = acc[:, h * HEAD_DIM:(h + 1) * HEAD_DIM]
            ms = jnp.mean(blk * blk, axis=-1, keepdims=True)
            o_ref[:, h * HEAD_DIM:(h + 1) * HEAD_DIM] = blk * lax.rsqrt(ms + RMS_EPS) * g_ref[...]
    else:
        (o_ref,) = refs
        o_ref[...] = acc


def _linear(x, w, tm, head_gain=None):
    r, k = x.shape
    n = w.shape[1]
    in_specs = [pl.BlockSpec((tm, k), lambda i: (i, 0)), pl.BlockSpec((k, n), lambda i: (0, 0))]
    args = [x, w]
    if head_gain is not None:
        in_specs.append(pl.BlockSpec((1, HEAD_DIM), lambda i: (0, 0)))
        args.append(head_gain.reshape(1, HEAD_DIM))
    return pl.pallas_call(
        functools.partial(_linear_kernel, headnorm=head_gain is not None),
        grid=(r // tm,),
        in_specs=in_specs,
        out_specs=pl.BlockSpec((tm, n), lambda i: (i, 0)),
        out_shape=jax.ShapeDtypeStruct((r, n), F32),
        compiler_params=_cparams(("parallel",), 2 * (tm * k * 2 + k * n * 2 + tm * n * 4) + 4 * tm * n * 4),
    )(*args)


def _linear_t_kernel(w_ref, x_ref, o_ref):
    o_ref[...] = _dot_nt(w_ref[...], x_ref[...])


def _linear_t(w_t, x, tm):
    r, k = x.shape
    n = w_t.shape[0]
    return pl.pallas_call(
        _linear_t_kernel,
        grid=(r // tm,),
        in_specs=[pl.BlockSpec((n, k), lambda i: (0, 0)), pl.BlockSpec((tm, k), lambda i: (i, 0))],
        out_specs=pl.BlockSpec((n, tm), lambda i: (0, i)),
        out_shape=jax.ShapeDtypeStruct((n, r), F32),
        compiler_params=_cparams(("parallel",)),
    )(w_t, x)


def _bias_table_kernel(bias_ref, o_ref):
    h = pl.program_id(0)
    shape = o_ref.shape[1:]
    dist = lax.broadcasted_iota(I32, shape, 1) - lax.broadcasted_iota(I32, shape, 0)
    o_ref[0] = jnp.where(dist >= 0, _bias_from_dist(dist, lambda b: bias_ref[h, b]), NEG)


def _bias_tables(rel_bias_t, seq):
    nh = rel_bias_t.shape[0]
    return pl.pallas_call(
        _bias_table_kernel,
        grid=(nh,),
        in_specs=[pl.BlockSpec(memory_space=pltpu.SMEM)],
        out_specs=pl.BlockSpec((1, ATT_TILE, seq), lambda h: (h, 0, 0)),
        out_shape=jax.ShapeDtypeStruct((nh, ATT_TILE, seq), F32),
        compiler_params=_cparams(("parallel",), 4 * ATT_TILE * seq * 4),
    )(rel_bias_t)


def _flash_attend(n_trips, block_of, mask_of, q_ref, kbf_ref, vt_ref, tb_ref, o_ref, s_ref, p_ref):
    qt = pl.program_id(2)
    heads = _head_slices()
    qs = [(q_ref[:, hs] * HEAD_DIM ** -0.5).astype(BF16) for hs in heads]

    def key_rows(blk):
        return pl.ds(pl.multiple_of(blk * ATT_TILE, ATT_TILE), ATT_TILE)

    first = block_of(0)
    for g, hs in enumerate(heads):
        s_ref[g] = _dot_nt(kbf_ref[key_rows(first), hs], qs[g])
        p_ref[g] = jnp.zeros(p_ref.shape[1:], p_ref.dtype)

    def body(i, carries):
        blk = block_of(i)
        prev_rows = key_rows(block_of(jnp.maximum(i - 1, 0)))
        next_rows = key_rows(block_of(jnp.minimum(i + 1, n_trips - 1)))
        delta = pl.multiple_of((qt - blk) * ATT_TILE, ATT_TILE)
        out = []
        for g, hs in enumerate(heads):
            m, l, acc = carries[g]
            s = s_ref[g] + tb_ref[g, :, pl.ds(delta, ATT_TILE)] + mask_of(blk, g)
            m_new = jnp.maximum(m, jnp.max(s, axis=0, keepdims=True))
            alpha = jnp.exp(m - m_new)
            p = jnp.exp(s - m_new)
            l = alpha * l + jnp.sum(p, axis=0, keepdims=True)
            acc = (acc + jnp.dot(vt_ref[hs, prev_rows], p_ref[g], preferred_element_type=F32)) * alpha
            p_ref[g] = p.astype(BF16)
            s_ref[g] = _dot_nt(kbf_ref[next_rows, hs], qs[g])
            out.append((m_new, l, acc))
        return tuple(out)

    m0 = jnp.full((1, ATT_TILE), NEG, F32)
    l0 = jnp.zeros((1, ATT_TILE), F32)
    a0 = jnp.zeros((HEAD_DIM, ATT_TILE), F32)
    carries = lax.fori_loop(0, n_trips, body, tuple((m0, l0, a0) for _ in heads))
    last_rows = key_rows(block_of(n_trips - 1))
    for g, hs in enumerate(heads):
        _, l, acc = carries[g]
        acc = acc + jnp.dot(vt_ref[hs, last_rows], p_ref[g], preferred_element_type=F32)
        o_ref[:, hs] = (acc / l).T


def _top_rows(s, k):
    n = s.shape[0]
    rows = lax.broadcasted_iota(I32, s.shape, 0)
    vals, picks = [], []
    for _ in range(k):
        m = jnp.max(s, axis=0, keepdims=True)
        first = jnp.min(jnp.where(s == m, rows, n), axis=0, keepdims=True)
        pick = rows == first
        s = jnp.where(pick, -jnp.inf, s)
        vals.append(m)
        picks.append(pick)
    return vals, picks


def _head_slices():
    return [slice(g * HEAD_DIM, (g + 1) * HEAD_DIM) for g in range(HEADS_PER_STEP)]


def _load_kv_scratch(k_ref, v_ref, kbf_ref, vt_ref):
    n_blk = k_ref.shape[0] // ATT_TILE
    for b in range(n_blk):
        sl = slice(b * ATT_TILE, (b + 1) * ATT_TILE)
        kbf_ref[sl, :] = k_ref[sl, :].astype(BF16)
        vt_ref[:, sl] = v_ref[sl, :].T.astype(BF16)


def _moba_kernel(q_ref, k_ref, v_ref, tb_ref, o_ref, kbf_ref, vt_ref, s_ref, p_ref, kmean_ref, selb_ref):
    qt = pl.program_id(2)
    n_blk = k_ref.shape[0] // ATT_TILE
    heads = _head_slices()

    @pl.when(qt == 0)
    def _():
        _load_kv_scratch(k_ref, v_ref, kbf_ref, vt_ref)
        for b in range(n_blk):
            kmean_ref[b:b + 1, :] = jnp.mean(k_ref[b * ATT_TILE:(b + 1) * ATT_TILE, :], axis=0, keepdims=True)

    for g, hs in enumerate(heads):
        gate = _dot_nt(kmean_ref[:, hs], q_ref[:, hs], precision=HIGHEST)
        blocks = lax.broadcasted_iota(I32, gate.shape, 0)
        past = blocks < qt
        _, picks = _top_rows(jnp.where(past, gate, NEG), min(MOBA_TOPK, n_blk))
        sel = blocks == qt
        for p in picks:
            sel = sel | (p & past)
        selb_ref[g] = jnp.where(sel, 0.0, NEG)

    _flash_attend(qt + 1, lambda i: jnp.where(i == 0, qt, i - 1),
                  lambda blk, g: selb_ref[g, pl.ds(blk, 1), :],
                  q_ref, kbf_ref, vt_ref, tb_ref, o_ref, s_ref, p_ref)


def _attn_specs(seq, nq, table_off):
    wide = HEADS_PER_STEP * HEAD_DIM
    tile = pl.BlockSpec((ATT_TILE, wide), lambda b, h, t: (b * nq + t, h))
    full = pl.BlockSpec((seq, wide), lambda b, h, t: (b, h))
    table = pl.BlockSpec((HEADS_PER_STEP, ATT_TILE, seq), lambda b, h, t: (h + table_off, 0, 0))
    scratch = [pltpu.VMEM((seq, wide), BF16), pltpu.VMEM((wide, seq), BF16),
               pltpu.VMEM((HEADS_PER_STEP, ATT_TILE, ATT_TILE), F32),
               pltpu.VMEM((HEADS_PER_STEP, ATT_TILE, ATT_TILE), BF16)]
    vmem = 2 * (2 * seq * wide * 4 + HEADS_PER_STEP * ATT_TILE * seq * 4) + 2 * seq * wide * 2
    return tile, full, table, scratch, vmem


def _moba_prompt(q, k, v, tables, batch, seq, n_heads):
    nq = seq // ATT_TILE
    n_blk = seq // MOBA_BLOCK
    tile, full, table, scratch, vmem = _attn_specs(seq, nq, 0)
    return pl.pallas_call(
        _moba_kernel,
        grid=(batch, n_heads // HEADS_PER_STEP, nq),
        in_specs=[tile, full, full, table],
        out_specs=tile,
        out_shape=jax.ShapeDtypeStruct(q.shape, F32),
        scratch_shapes=scratch + [pltpu.VMEM((n_blk, HEADS_PER_STEP * HEAD_DIM), F32),
                                  pltpu.VMEM((HEADS_PER_STEP, n_blk, ATT_TILE), F32)],
        compiler_params=_cparams(("parallel", "parallel", "arbitrary"), vmem + 8 * 1024 * 1024),
    )(q, k, v, tables)


def _sortable(x):
    bits = lax.bitcast_convert_type(x, I32)
    return bits ^ ((bits >> 31) & 0x7FFFFFFF)


def _dsa_index_kernel(qi_ref, wt_ref, ki_ref, o_ref, u_ref, *, n_sel, n_idx_heads):
    qt = pl.program_id(1)
    nq = ki_ref.shape[0] // ATT_TILE
    w = wt_ref[...] * (IDX_DIM ** -0.5 * n_idx_heads ** -0.5)
    qi = qi_ref[...].astype(BF16)
    rows = lax.broadcasted_iota(I32, (ATT_TILE, ATT_TILE), 0)
    cols = lax.broadcasted_iota(I32, (ATT_TILE, ATT_TILE), 1)

    def score_body(kb, _):
        kk = pl.multiple_of(kb * ATT_TILE, ATT_TILE)
        kblk = ki_ref[pl.ds(kk, ATT_TILE), :].astype(BF16)
        sc = jnp.zeros((ATT_TILE, ATT_TILE), F32)
        for h in range(n_idx_heads):
            sh = _dot_nt(kblk, qi[:, h * IDX_DIM:(h + 1) * IDX_DIM])
            sc = sc + jnp.maximum(sh, 0.0) * w[h:h + 1, :]
        sc = jnp.where((kb < qt) | (rows <= cols), sc, NEG)
        u_ref[pl.ds(kk, ATT_TILE), :] = _sortable(sc)
        return 0

    lax.fori_loop(0, qt + 1, score_body, 0)

    def bit_body(i, t):
        cand = t + lax.shift_left(jnp.int32(1), 31 - i)

        def count_body(kb, cnt):
            kk = pl.multiple_of(kb * ATT_TILE, ATT_TILE)
            hit = (u_ref[pl.ds(kk, ATT_TILE), :] >= cand).astype(I32)
            return cnt + jnp.sum(hit, axis=0, keepdims=True)

        cnt = lax.fori_loop(0, qt + 1, count_body, jnp.zeros((1, ATT_TILE), I32))
        return jnp.where(cnt >= n_sel, cand, t)

    thr = lax.fori_loop(0, 32, bit_body, jnp.full((1, ATT_TILE), -2 ** 31, I32))

    def out_body(kb, _):
        kk = pl.multiple_of(kb * ATT_TILE, ATT_TILE)
        keep = (u_ref[pl.ds(kk, ATT_TILE), :] >= thr) & ((kb < qt) | (rows <= cols))
        o_ref[0, pl.ds(kk, ATT_TILE), :] = jnp.where(keep, 0.0, NEG).astype(o_ref.dtype)
        return 0

    def future_body(kb, _):
        kk = pl.multiple_of(kb * ATT_TILE, ATT_TILE)
        o_ref[0, pl.ds(kk, ATT_TILE), :] = jnp.full((ATT_TILE, ATT_TILE), NEG, o_ref.dtype)
        return 0

    lax.fori_loop(0, qt + 1, out_body, 0)
    lax.fori_loop(qt + 1, nq, future_body, 0)


def _dsa_index(qi, wi_t, ki, batch, seq, n_idx_heads):
    nq = seq // ATT_TILE
    n_sel = min(DSA_TOPK, seq // 4)
    return pl.pallas_call(
        functools.partial(_dsa_index_kernel, n_sel=n_sel, n_idx_heads=n_idx_heads),
        grid=(batch, nq),
        in_specs=[pl.BlockSpec((ATT_TILE, qi.shape[1]), lambda b, t: (b * nq + t, 0)),
                  pl.BlockSpec((n_idx_heads, ATT_TILE), lambda b, t: (0, b * nq + t)),
                  pl.BlockSpec((seq, IDX_DIM), lambda b, t: (b, 0))],
        out_specs=pl.BlockSpec((1, seq, ATT_TILE), lambda b, t: (b, 0, t)),
        out_shape=jax.ShapeDtypeStruct((batch, seq, seq), BF16),
        scratch_shapes=[pltpu.VMEM((seq, ATT_TILE), I32)],
        compiler_params=_cparams(("parallel", "arbitrary"), 32 * 1024 * 1024),
    )(qi, wi_t, ki)


def _dsa_attn_kernel(q_ref, k_ref, v_ref, tb_ref, mask_ref, o_ref, kbf_ref, vt_ref, s_ref, p_ref):
    qt = pl.program_id(2)

    @pl.when(qt == 0)
    def _():
        _load_kv_scratch(k_ref, v_ref, kbf_ref, vt_ref)

    def mask_of(blk, g):
        return mask_ref[0, pl.ds(pl.multiple_of(blk * ATT_TILE, ATT_TILE), ATT_TILE), :].astype(F32)

    _flash_attend(qt + 1, lambda i: i, mask_of, q_ref, kbf_ref, vt_ref, tb_ref, o_ref, s_ref, p_ref)


def _dsa_prompt(q, k, v, tables, mask, batch, seq, n_heads, head_off):
    nq = seq // ATT_TILE
    tile, full, table, scratch, vmem = _attn_specs(seq, nq, head_off // HEADS_PER_STEP)
    return pl.pallas_call(
        _dsa_attn_kernel,
        grid=(batch, n_heads // HEADS_PER_STEP, nq),
        in_specs=[tile, full, full, table, pl.BlockSpec((1, seq, ATT_TILE), lambda b, h, t: (b, 0, t))],
        out_specs=tile,
        out_shape=jax.ShapeDtypeStruct(q.shape, F32),
        scratch_shapes=scratch,
        compiler_params=_cparams(("parallel", "parallel", "arbitrary"),
                                 vmem + 2 * seq * ATT_TILE * 2 + 8 * 1024 * 1024),
    )(q, k, v, tables, mask)


def _merge_kernel(oa_ref, ob_ref, ga_ref, gb_ref, wa_ref, wb_ref, o_ref):
    bra = jnp.dot(oa_ref[...].astype(BF16), wa_ref[...], preferred_element_type=F32)
    brb = jnp.dot(ob_ref[...].astype(BF16), wb_ref[...], preferred_element_type=F32)
    o_ref[...] = (jax.nn.sigmoid(ga_ref[...]) * bra + jax.nn.sigmoid(gb_ref[...]) * brb).astype(o_ref.dtype)


def _merge(out_a, out_b, ga, gb, w_pa, w_pb, tm):
    r, w = out_a.shape
    d = ga.shape[1]
    row = lambda n: pl.BlockSpec((tm, n), lambda i: (i, 0))
    full = lambda a: pl.BlockSpec(a.shape, lambda i: (0, 0))
    return pl.pallas_call(
        _merge_kernel,
        grid=(r // tm,),
        in_specs=[row(w), row(w), row(d), row(d), full(w_pa), full(w_pb)],
        out_specs=row(d),
        out_shape=jax.ShapeDtypeStruct((r, d), BF16),
        compiler_params=_cparams(("parallel",), 4 * w * d * 2 + 8 * tm * d * 4),
    )(out_a, out_b, ga, gb, w_pa, w_pb)


def _outproj_kernel(mg_ref, w_ref, x_ref, g1_ref, ng_ref, sc_ref, sh_ref, h_ref, hm_ref):
    h = x_ref[...] + g1_ref[0] * jnp.dot(mg_ref[...], w_ref[...], preferred_element_type=F32)
    h_ref[...] = h
    y = h * lax.rsqrt(jnp.mean(h * h, axis=-1, keepdims=True) + RMS_EPS) * ng_ref[...]
    hm_ref[...] = (y * (1.0 + sc_ref[0]) + sh_ref[0]).astype(hm_ref.dtype)


def _outproj(merged, w_out, x, g1, norm_g, sc2, sh2, tm, tiles_per_group):
    r, d = x.shape
    row = pl.BlockSpec((tm, d), lambda i: (i, 0))
    ms = lambda m: _mod_spec(m, tm, tiles_per_group)
    return pl.pallas_call(
        _outproj_kernel,
        grid=(r // tm,),
        in_specs=[row, pl.BlockSpec((d, d), lambda i: (0, 0)), row, ms(g1),
                  pl.BlockSpec((1, d), lambda i: (0, 0)), ms(sc2), ms(sh2)],
        out_specs=[row, row],
        out_shape=[jax.ShapeDtypeStruct((r, d), F32), jax.ShapeDtypeStruct((r, d), BF16)],
        compiler_params=_cparams(("parallel",), 4 * d * d * 2 + 12 * tm * d * 4),
    )(merged, w_out, x, g1, norm_g.reshape(1, d), sc2, sh2)


def _peer_pairs(k):
    return [(i, j) for i in range(k) for j in range(k) if (i + 1) * (j + 1) <= k]


def _peer_route_kernel(q_ref, k1_ref, k2_ref, e1_ref, e2_ref, tau_ref):
    n_heads = k1_ref.shape[0]
    dk = k1_ref.shape[2]
    kk = PEER_TOPK + 1
    for h in range(n_heads):
        q1 = q_ref[:, (2 * h) * dk:(2 * h + 1) * dk]
        q2 = q_ref[:, (2 * h + 1) * dk:(2 * h + 2) * dk]
        s1 = _dot_nt(k1_ref[h], q1, precision=HIGHEST)
        s2 = _dot_nt(k2_ref[h], q2, precision=HIGHEST)
        v1, _ = _top_rows(s1, kk)
        v2, _ = _top_rows(s2, kk)
        cand = jnp.concatenate([v1[i] + v2[j] for i, j in _peer_pairs(kk)], axis=0)
        c, _ = _top_rows(cand, kk)
        top = c[0]
        z = sum(jnp.exp(ck - top) for ck in c[:PEER_TOPK])
        inv_z = 1.0 / z
        e1_ref[h] = jnp.exp(s1 - v1[0]) * inv_z
        e2_ref[h] = jnp.exp(s2 - v2[0])
        tau_ref[h:h + 1, :] = jnp.exp(0.5 * (c[PEER_TOPK - 1] + c[PEER_TOPK]) - top) * inv_z


def _peer_route(qp, keys1, keys2, tm):
    r, d = qp.shape
    nh, nk, dk = keys1.shape
    return pl.pallas_call(
        _peer_route_kernel,
        grid=(r // tm,),
        in_specs=[pl.BlockSpec((tm, d), lambda i: (i, 0)),
                  pl.BlockSpec((nh, nk, dk), lambda i: (0, 0, 0)),
                  pl.BlockSpec((nh, nk, dk), lambda i: (0, 0, 0))],
        out_specs=[pl.BlockSpec((nh, nk, tm), lambda i: (0, 0, i)),
                   pl.BlockSpec((nh, nk, tm), lambda i: (0, 0, i)),
                   pl.BlockSpec((nh, tm), lambda i: (0, i))],
        out_shape=[jax.ShapeDtypeStruct((nh, nk, r), F32), jax.ShapeDtypeStruct((nh, nk, r), F32),
                   jax.ShapeDtypeStruct((nh, r), F32)],
        compiler_params=_cparams(("parallel",), 32 * 1024 * 1024),
    )(qp, keys1, keys2)


def _gelu_tanh(x):
    return 0.5 * x * (1.0 + jnp.tanh(math.sqrt(2.0 / math.pi) * (x + 0.044715 * (x * x * x))))


def _peer_dense_kernel(hm_ref, u_ref, v_ref, e1_ref, e2_ref, tau_ref, o_ref, act_a, act_b):
    j = pl.program_id(1)
    n_tiles = pl.num_programs(1) - 1
    te = u_ref.shape[0]
    n_heads = e1_ref.shape[0]
    rows_per_tile = te // PEER_NKEYS

    def project(dst):
        dst[...] = _dot_nt(u_ref[...], hm_ref[...])

    def mix(src):
        rows_per_dot = min(rows_per_tile, 2)
        total = None
        for r0 in range(0, rows_per_tile, rows_per_dot):
            pieces = []
            for r in range(r0, r0 + rows_per_dot):
                i1 = (j - 1) * rows_per_tile + r
                g = jnp.zeros((PEER_NKEYS, src.shape[1]), F32)
                for h in range(n_heads):
                    w = e2_ref[h] * e1_ref[h, pl.ds(i1, 1), :]
                    g = g + jnp.where(w >= tau_ref[h:h + 1, :], w, 0.0)
                a_t = g * _gelu_tanh(src[r * PEER_NKEYS:(r + 1) * PEER_NKEYS, :])
                pieces.append(a_t.T.astype(BF16))
            a = jnp.concatenate(pieces, axis=1) if rows_per_dot > 1 else pieces[0]
            part = jnp.dot(a, v_ref[r0 * PEER_NKEYS:(r0 + rows_per_dot) * PEER_NKEYS, :],
                           preferred_element_type=F32)
            total = part if total is None else total + part
        o_ref[...] += total

    @pl.when(j == 0)
    def _():
        o_ref[...] = jnp.zeros_like(o_ref)
        project(act_a)

    @pl.when((j > 0) & (j < n_tiles) & (j % 2 == 0))
    def _():
        mix(act_b)
        project(act_a)

    @pl.when((j > 0) & (j < n_tiles) & (j % 2 == 1))
    def _():
        mix(act_a)
        project(act_b)

    @pl.when((j == n_tiles) & (j % 2 == 0))
    def _():
        mix(act_b)

    @pl.when((j == n_tiles) & (j % 2 == 1))
    def _():
        mix(act_a)


def _peer_dense(hm, u, v, e1n, e2, tau, tm, te):
    r, d = hm.shape
    n_tiles = u.shape[0] // te
    nh, nk, _ = e1n.shape
    return pl.pallas_call(
        _peer_dense_kernel,
        grid=(r // tm, n_tiles + 1),
        in_specs=[pl.BlockSpec((tm, d), lambda i, j: (i, 0)),
                  pl.BlockSpec((te, d), lambda i, j: (jnp.minimum(j, n_tiles - 1), 0)),
                  pl.BlockSpec((te, d), lambda i, j: (jnp.maximum(j - 1, 0), 0)),
                  pl.BlockSpec((nh, nk, tm), lambda i, j: (0, 0, i)),
                  pl.BlockSpec((nh, nk, tm), lambda i, j: (0, 0, i)),
                  pl.BlockSpec((nh, tm), lambda i, j: (0, i))],
        out_specs=pl.BlockSpec((tm, d), lambda i, j: (i, 0)),
        out_shape=jax.ShapeDtypeStruct((r, d), F32),
        scratch_shapes=[pltpu.VMEM((te, tm), F32), pltpu.VMEM((te, tm), F32)],
        compiler_params=_cparams(("parallel", "arbitrary"),
                                 2 * (tm * d * 2 + 2 * te * d * 2 + 2 * nh * nk * tm * 4 + tm * d * 4)
                                 + 10 * te * tm * 4),
    )(hm, u, v, e1n, e2, tau)


def _residual_kernel(h_ref, p_ref, g2_ref, o_ref):
    o_ref[...] = h_ref[...] + g2_ref[0] * p_ref[...]


def _residual(h, p, g2, tm, tiles_per_group):
    r, d = h.shape
    row = pl.BlockSpec((tm, d), lambda i: (i, 0))
    return pl.pallas_call(
        _residual_kernel,
        grid=(r // tm,),
        in_specs=[row, row, _mod_spec(g2, tm, tiles_per_group)],
        out_specs=row,
        out_shape=jax.ShapeDtypeStruct((r, d), F32),
        compiler_params=_cparams(("parallel",), 8 * tm * d * 4),
    )(h, p, g2)


def _sample_scan_kernel(pt_ref, qa_ref, qi_ref, wi_ref, kinew_ref, ka_hbm, kidx_hbm, blk_ref, score_ref,
                        kbuf, ibuf, ksum_ref, ksem, isem, *, n_idx_heads, n_heads):
    ns, n_pages = pt_ref.shape
    total = ns * n_pages
    pages_per_blk = MOBA_BLOCK // PAGE_SIZE
    n_blk = ksum_ref.shape[0]
    n_bufs = kbuf.shape[0]

    def page_copies(g, slot):
        s = g // n_pages
        page = pt_ref[s, g - s * n_pages]
        return (pltpu.make_async_copy(ka_hbm.at[0, page], kbuf.at[slot], ksem.at[slot]),
                pltpu.make_async_copy(kidx_hbm.at[0, page], ibuf.at[slot], isem.at[slot]))

    for g in range(n_bufs):
        for c in page_copies(g, g):
            c.start()

    def body(g, _):
        slot = g % n_bufs
        s = g // n_pages
        p = g - s * n_pages
        for c in page_copies(g, slot):
            c.wait()

        @pl.when(p == 0)
        def _():
            ksum_ref[...] = jnp.zeros_like(ksum_ref)
            score_ref[s, pl.ds(n_pages, 8), :] = jnp.full((8, PAGE_SIZE), -jnp.inf, F32)

        ksum_ref[p // pages_per_blk] += jnp.sum(kbuf[slot], axis=0)

        w = wi_ref[s] * (IDX_DIM ** -0.5 * n_idx_heads ** -0.5)
        qi = qi_ref[s]
        sc = jnp.maximum(_dot_nt(qi.astype(BF16), ibuf[slot].astype(BF16)), 0.0) * w
        score_ref[s, pl.ds(p, 1), :] = jnp.sum(sc, axis=0, keepdims=True)

        @pl.when(g + n_bufs < total)
        def _():
            for c in page_copies(g + n_bufs, slot):
                c.start()

        @pl.when(p == n_pages - 1)
        def _():
            kn = kinew_ref[s].astype(BF16).astype(F32)
            dots = jnp.sum(qi.astype(BF16).astype(F32) * kn, axis=-1, keepdims=True)
            new_score = jnp.sum(jnp.maximum(dots, 0.0) * w, axis=0, keepdims=True)
            lane = lax.broadcasted_iota(I32, (1, PAGE_SIZE), 1)
            score_ref[s, pl.ds(n_pages, 1), :] = jnp.where(lane == 0, new_score, -jnp.inf)

            gate = jnp.sum(ksum_ref[...] * (qa_ref[s] * (1.0 / MOBA_BLOCK)), axis=-1)
            rows = lax.broadcasted_iota(I32, gate.shape, 0)
            out = jnp.zeros(blk_ref.shape[1:], I32)
            orow = lax.broadcasted_iota(I32, out.shape, 0)
            for t in range(MOBA_TOPK):
                m = jnp.max(gate, axis=0, keepdims=True)
                first = jnp.min(jnp.where(gate == m, rows, n_blk), axis=0, keepdims=True)
                gate = jnp.where(rows == first, -jnp.inf, gate)
                padded = jnp.concatenate([first, jnp.zeros((1, out.shape[1] - n_heads), I32)], axis=1)
                out = jnp.where(orow == t, padded, out)
            blk_ref[s] = out
        return 0

    lax.fori_loop(0, total, body, 0)


def _sample_scan(page_table, cache_ka, cache_kidx, qa, qi, wi, ki_new, n_heads, n_idx_heads):
    ns, n_pages = page_table.shape
    n_blk = n_pages * PAGE_SIZE // MOBA_BLOCK
    n_bufs = min(SCAN_BUFS, ns * n_pages)
    whole = lambda a: pl.BlockSpec(a.shape, lambda i, pt: (0,) * a.ndim)
    hbm = pl.BlockSpec(memory_space=pl.ANY)
    out_shape = [jax.ShapeDtypeStruct((ns, 8, LANES), I32),
                 jax.ShapeDtypeStruct((ns, n_pages + 8, PAGE_SIZE), F32)]
    grid_spec = pltpu.PrefetchScalarGridSpec(
        num_scalar_prefetch=1,
        grid=(1,),
        in_specs=[whole(qa), whole(qi), whole(wi), whole(ki_new), hbm, hbm],
        out_specs=[whole(o) for o in out_shape],
        scratch_shapes=[pltpu.VMEM((n_bufs, PAGE_SIZE, n_heads, HEAD_DIM), F32),
                        pltpu.VMEM((n_bufs, PAGE_SIZE, IDX_DIM), F32),
                        pltpu.VMEM((n_blk, n_heads, HEAD_DIM), F32),
                        pltpu.SemaphoreType.DMA((n_bufs,)), pltpu.SemaphoreType.DMA((n_bufs,))])
    return pl.pallas_call(
        functools.partial(_sample_scan_kernel, n_idx_heads=n_idx_heads, n_heads=n_heads),
        grid_spec=grid_spec,
        out_shape=out_shape,
        compiler_params=_cparams(("arbitrary",), 24 * 1024 * 1024),
    )(page_table, qa, qi, wi, ki_new, cache_ka, cache_kidx)


def _sample_topk_kernel(score_ref, idx_ref, *, n_sel, n_valid):
    s = score_ref[...]
    n = s.shape[1]
    lane = lax.broadcasted_iota(I32, s.shape, 1)
    s = jnp.where(lane < n_valid, s, -jnp.inf)
    out_lane = lax.broadcasted_iota(I32, idx_ref.shape, 1)

    def body(t, carry):
        s, out = carry
        m = jnp.max(s, axis=1, keepdims=True)
        first = jnp.min(jnp.where(s == m, lane, n), axis=1, keepdims=True)
        s = jnp.where(lane == first, -jnp.inf, s)
        return s, jnp.where(out_lane == t, first, out)

    _, out = lax.fori_loop(0, n_sel, body, (s, jnp.zeros(idx_ref.shape, I32)))
    idx_ref[...] = out


def _sample_topk(score, n_sel, n_valid):
    ns, n = score.shape
    return pl.pallas_call(
        functools.partial(_sample_topk_kernel, n_sel=n_sel, n_valid=n_valid),
        out_shape=jax.ShapeDtypeStruct((ns, n_sel), I32),
        compiler_params=_cparams(None, 8 * ns * n * 4),
    )(score)


def _sample_moba_kernel(pt_ref, blk_ref, bias_ref, q_ref, kn_ref, vn_ref, ck_hbm, cv_hbm, o_ref,
                        kbuf, vbuf, sem, *, past, n_heads):
    s_id = pl.program_id(0)
    pages_per_blk = MOBA_BLOCK // PAGE_SIZE
    slabs = [(h, t, half) for h in range(n_heads) for t in range(MOBA_TOPK) for half in range(pages_per_blk)]

    def slab_copies(i):
        h, t, half = slabs[i]
        page = pt_ref[s_id, blk_ref[s_id, h * MOBA_TOPK + t] * pages_per_blk + half]
        return (pltpu.make_async_copy(ck_hbm.at[0, page, pl.ds(0, PAGE_SIZE), h], kbuf.at[i], sem.at[0]),
                pltpu.make_async_copy(cv_hbm.at[0, page, pl.ds(0, PAGE_SIZE), h], vbuf.at[i], sem.at[1]))

    for i in range(len(slabs)):
        for c in slab_copies(i):
            c.start()
    for i in range(len(slabs)):
        for c in slab_copies(i):
            c.wait()

    lane = lax.broadcasted_iota(I32, (1, PAGE_SIZE), 1)
    outs = []
    for h in range(n_heads):
        q8 = jnp.broadcast_to(q_ref[0, h:h + 1, :] * HEAD_DIM ** -0.5, (8, HEAD_DIM)).astype(BF16)
        kn = kn_ref[0, h:h + 1, :].astype(BF16).astype(F32)
        s_own = jnp.sum(q8[0:1].astype(F32) * kn, axis=-1, keepdims=True) + bias_ref[h, 0]
        logits = []
        m = s_own
        for i, (hh, t, half) in enumerate(slabs):
            if hh != h:
                continue
            key0 = blk_ref[s_id, h * MOBA_TOPK + t] * MOBA_BLOCK + half * PAGE_SIZE
            dist = past - (key0 + lane)
            s = _dot_nt(q8, kbuf[i].astype(BF16))[0:1] + _bias_from_dist(dist, lambda b: bias_ref[h, b])
            s = jnp.where(dist >= 0, s, NEG)
            logits.append((i, s))
            m = jnp.maximum(m, jnp.max(s, axis=-1, keepdims=True))
        p_own = jnp.exp(s_own - m)
        l = p_own
        acc = p_own * vn_ref[0, h:h + 1, :].astype(BF16).astype(F32)
        for i, s in logits:
            p = jnp.exp(s - m)
            l = l + jnp.sum(p, axis=-1, keepdims=True)
            p8 = jnp.broadcast_to(p, (8, PAGE_SIZE)).astype(BF16)
            acc = acc + jnp.dot(p8, vbuf[i].astype(BF16), preferred_element_type=F32)[0:1]
        outs.append(acc / l)
    o_ref[0] = jnp.concatenate(outs, axis=0)


def _sample_moba(page_table, blk_flat, rel_bias_t, qa, cache_ka, cache_va, ka_new, va_new, n_heads):
    ns, n_pages = page_table.shape
    n_slabs = n_heads * MOBA_TOPK * (MOBA_BLOCK // PAGE_SIZE)
    heads = pl.BlockSpec((1, n_heads, HEAD_DIM), lambda s, pt, blk: (s, 0, 0))
    hbm = pl.BlockSpec(memory_space=pl.ANY)
    grid_spec = pltpu.PrefetchScalarGridSpec(
        num_scalar_prefetch=2,
        grid=(ns,),
        in_specs=[pl.BlockSpec(memory_space=pltpu.SMEM), heads, heads, heads, hbm, hbm],
        out_specs=heads,
        scratch_shapes=[pltpu.VMEM((n_slabs, PAGE_SIZE, HEAD_DIM), F32),
                        pltpu.VMEM((n_slabs, PAGE_SIZE, HEAD_DIM), F32),
                        pltpu.SemaphoreType.DMA((2,))])
    return pl.pallas_call(
        functools.partial(_sample_moba_kernel, past=n_pages * PAGE_SIZE, n_heads=n_heads),
        grid_spec=grid_spec,
        out_shape=jax.ShapeDtypeStruct(qa.shape, F32),
        compiler_params=_cparams(("arbitrary",), 24 * 1024 * 1024),
    )(page_table, blk_flat, rel_bias_t, qa, ka_new, va_new, cache_ka, cache_va)


def _sample_dsa_kernel(pt_ref, sel_ref, bias_ref, selv_ref, q_ref, kn_ref, vn_ref, ck_ref, cv_ref, o_ref,
                       kbuf, vbuf, sem, *, past, n_heads, head_off):
    s_id = pl.program_id(0)
    n_sel = kbuf.shape[0]

    def row_copies(r):
        idx = sel_ref[s_id, r]
        page = pt_ref[s_id, jnp.minimum(idx, past - 1) // PAGE_SIZE]
        pos = idx % PAGE_SIZE
        dst_k, dst_v = kbuf.at[pl.ds(r, 1)], vbuf.at[pl.ds(r, 1)]
        cached = (pltpu.make_async_copy(ck_ref.at[0, page, pl.ds(pos, 1)], dst_k, sem.at[0]),
                  pltpu.make_async_copy(cv_ref.at[0, page, pl.ds(pos, 1)], dst_v, sem.at[1]))
        fresh = (pltpu.make_async_copy(kn_ref.at[pl.ds(s_id, 1)], dst_k, sem.at[0]),
                 pltpu.make_async_copy(vn_ref.at[pl.ds(s_id, 1)], dst_v, sem.at[1]))
        return idx, cached, fresh

    def start(r, _):
        idx, cached, fresh = row_copies(r)

        @pl.when(idx < past)
        def _():
            for c in cached:
                c.start()

        @pl.when(idx >= past)
        def _():
            for c in fresh:
                c.start()
        return 0

    def wait(r, _):
        idx, cached, fresh = row_copies(r)

        @pl.when(idx < past)
        def _():
            for c in cached:
                c.wait()

        @pl.when(idx >= past)
        def _():
            for c in fresh:
                c.wait()
        return 0

    lax.fori_loop(0, n_sel, start, 0)
    lax.fori_loop(0, n_sel, wait, 0)

    dist = past - selv_ref[0]
    outs = []
    for h in range(n_heads):
        q8 = jnp.broadcast_to(q_ref[0, h:h + 1, :] * HEAD_DIM ** -0.5, (8, HEAD_DIM)).astype(BF16)
        s = _dot_nt(q8, kbuf[:, h, :].astype(BF16))[0:1]
        s = s + _bias_from_dist(dist, lambda b: bias_ref[h + head_off, b])
        s = jnp.where(dist >= 0, s, NEG)
        p = jnp.exp(s - jnp.max(s, axis=-1, keepdims=True))
        l = jnp.sum(p, axis=-1, keepdims=True)
        p8 = jnp.broadcast_to(p, (8, n_sel)).astype(BF16)
        outs.append(jnp.dot(p8, vbuf[:, h, :].astype(BF16), preferred_element_type=F32)[0:1] / l)
    o_ref[0] = jnp.concatenate(outs, axis=0)


def _sample_dsa(page_table, sel, rel_bias_t, qb, kb_new, vb_new, cache_kb, cache_vb, n_heads, head_off):
    ns, n_pages = page_table.shape
    n_sel = sel.shape[1]
    row = pl.BlockSpec((1, n_heads, HEAD_DIM), lambda s, pt, sl: (s, 0, 0))
    hbm = pl.BlockSpec(memory_space=pl.ANY)
    grid_spec = pltpu.PrefetchScalarGridSpec(
        num_scalar_prefetch=2,
        grid=(ns,),
        in_specs=[pl.BlockSpec(memory_space=pltpu.SMEM),
                  pl.BlockSpec((1, 1, n_sel), lambda s, pt, sl: (s, 0, 0)),
                  row, hbm, hbm, hbm, hbm],
        out_specs=row,
        scratch_shapes=[pltpu.VMEM((n_sel, n_heads, HEAD_DIM), F32), pltpu.VMEM((n_sel, n_heads, HEAD_DIM), F32),
                        pltpu.SemaphoreType.DMA((2,))])
    return pl.pallas_call(
        functools.partial(_sample_dsa_kernel, past=n_pages * PAGE_SIZE, n_heads=n_heads, head_off=head_off),
        grid_spec=grid_spec,
        out_shape=jax.ShapeDtypeStruct(qb.shape, F32),
        compiler_params=_cparams(("arbitrary",)),
    )(page_table, sel, rel_bias_t, sel.reshape(ns, 1, n_sel), qb, kb_new, vb_new, cache_kb, cache_vb)


def _split_weights(w_in, w_a, w_b, n_idx_heads, d):
    sizes = (w_a, w_a, w_a, w_b, w_b, w_b, n_idx_heads * IDX_DIM, IDX_DIM, n_idx_heads, d, d)
    offs = np.concatenate([[0], np.cumsum(sizes)])
    return [w_in[:, int(offs[i]):int(offs[i + 1])].astype(BF16) for i in range(len(sizes))]


def _layer_rows(x, mods, tm, tiles_per_group, lw, attend):
    (norm_mix_g, w_sec, qn_a, kn_a, qn_b, kn_b, w_pa, w_pb, w_out, norm_ffn_g, w_qp, keys1, keys2, pu, pv) = lw
    sh1, sc1, g1, sh2, sc2, g2 = mods
    r, d = x.shape
    xm = _prenorm(x, norm_mix_g, sc1, sh1, tm, tiles_per_group)
    w_qa, w_ka, w_va, w_qb, w_kb, w_vb, w_qi, w_ki, w_wi, w_ga, w_gb = w_sec
    qa = _linear(xm, w_qa, tm, qn_a)
    ka = _linear(xm, w_ka, tm, kn_a)
    va = _linear(xm, w_va, tm)
    qb = _linear(xm, w_qb, tm, qn_b)
    kb = _linear(xm, w_kb, tm, kn_b)
    vb = _linear(xm, w_vb, tm)
    qi = _linear(xm, w_qi, tm)
    ki = _linear(xm, w_ki, tm)
    wi_t = _linear_t(w_wi.T, xm, tm)
    ga = _linear(xm, w_ga, tm)
    gb = _linear(xm, w_gb, tm)
    out_a, out_b = attend(qa, ka, va, qb, kb, vb, qi, ki, wi_t)
    merged = _merge(out_a, out_b, ga, gb, w_pa, w_pb, tm)
    h, hm = _outproj(merged, w_out, x, g1, norm_ffn_g, sc2, sh2, tm, tiles_per_group)
    qp = _linear(hm, w_qp, tm)
    tm_peer = min(r, 512)
    e1n, e2, tau = _peer_route(qp, keys1, keys2, min(r, 256))
    peer = _peer_dense(hm, pu, pv, e1n, e2, tau, tm_peer, 512)
    y = _residual(h, peer, g2, tm, tiles_per_group)
    return y, (ka, va, kb, vb, ki)


def kernel(x_prompt, x_sample, cache_k_a, cache_v_a, cache_k_b, cache_v_b, cache_k_idx, page_table, c_prompt, c_sample, rel_bias, w_ada, b_ada, norm_mix_g, w_in, qnorm_a, knorm_a, qnorm_b, knorm_b, w_proj_a, w_proj_b, w_out, norm_ffn_g, w_q_peer, peer_keys1, peer_keys2, peer_u, peer_v):
    batch, seq, d = x_prompt.shape
    ns, dec_seq, _ = x_sample.shape
    depth = w_ada.shape[0]
    assert depth == 1 and dec_seq == 1 and seq % ATT_TILE == 0
    n_heads = d // 256
    n_idx_heads = d // 128
    w_att = n_heads * HEAD_DIM
    n_pages = page_table.shape[1]
    past = n_pages * PAGE_SIZE
    ns_pad = -(-ns // LANES) * LANES

    n_c = batch + ns_pad
    n_c_pad = -(-n_c // 8) * 8
    c_all = jnp.zeros((n_c_pad, d), F32).at[:batch].set(c_prompt).at[batch:batch + ns].set(c_sample)
    mod = _ada_mod(c_all, w_ada[0], b_ada[0])
    mods_p = tuple(mod[:batch, i * d:(i + 1) * d].reshape(batch, 1, d) for i in range(6))
    mods_s = tuple(mod[batch:batch + ns_pad, i * d:(i + 1) * d].reshape(1, ns_pad, d) for i in range(6))

    lw = (norm_mix_g[0], _split_weights(w_in[0], w_att, w_att, n_idx_heads, d),
          qnorm_a[0], knorm_a[0], qnorm_b[0], knorm_b[0],
          w_proj_a[0].astype(BF16), w_proj_b[0].astype(BF16), w_out[0].astype(BF16), norm_ffn_g[0],
          w_q_peer[0].astype(BF16), peer_keys1[0], peer_keys2[0],
          peer_u[0].astype(BF16), peer_v[0].astype(BF16))
    rel_bias_t = rel_bias.T
    tables = _bias_tables(rel_bias_t, seq)

    def attend_prompt(qa, ka, va, qb, kb, vb, qi, ki, wi_t):
        out_a = _moba_prompt(qa, ka, va, tables, batch, seq, n_heads)
        mask = _dsa_index(qi, wi_t, ki, batch, seq, n_idx_heads)
        out_b = _dsa_prompt(qb, kb, vb, tables, mask, batch, seq, n_heads, n_heads)
        return out_a, out_b

    tm_p = 512 if seq % 512 == 0 else ATT_TILE
    y_p, new_p = _layer_rows(x_prompt.reshape(batch * seq, d), mods_p, tm_p, seq // tm_p, lw, attend_prompt)

    def attend_sample(qa, ka, va, qb, kb, vb, qi, ki, wi_t):
        hd = lambda a: a[:ns].reshape(ns, n_heads, HEAD_DIM)
        wi = wi_t[:, :ns].T.reshape(ns, n_idx_heads, 1)
        blk, score = _sample_scan(page_table, cache_k_a, cache_k_idx, hd(qa),
                                  qi[:ns].reshape(ns, n_idx_heads, IDX_DIM), wi,
                                  ki[:ns].reshape(ns, 1, IDX_DIM), n_heads, n_idx_heads)
        blk_flat = blk[:, :MOBA_TOPK, :n_heads].transpose(0, 2, 1).reshape(ns, n_heads * MOBA_TOPK)
        out_a = _sample_moba(page_table, blk_flat, rel_bias_t, hd(qa), cache_k_a, cache_v_a, hd(ka), hd(va),
                             n_heads)
        n_sel = min(DSA_TOPK, (past + 1) // 4)
        sel = _sample_topk(score.reshape(ns, -1), n_sel, past + 1)
        out_b = _sample_dsa(page_table, sel, rel_bias_t, hd(qb), hd(kb), hd(vb), cache_k_b, cache_v_b,
                            n_heads, n_heads)
        pad = lambda a: jnp.zeros((ns_pad, w_att), F32).at[:ns].set(a.reshape(ns, w_att))
        return pad(out_a), pad(out_b)

    x_s = jnp.zeros((ns_pad, d), F32).at[:ns].set(x_sample.reshape(ns, d))
    y_s, new_s = _layer_rows(x_s, mods_s, ns_pad, 1, lw, attend_sample)

    def state(new, n_seq, t):
        ka, va, kb, vb, ki = new
        hd = lambda a: a[:n_seq * t].reshape(1, n_seq, t, n_heads, HEAD_DIM)
        return hd(ka), hd(va), hd(kb), hd(vb), ki[:n_seq * t].reshape(1, n_seq, t, IDX_DIM)

    return ((y_p.reshape(batch, seq, d), y_s[:ns].reshape(ns, 1, d))
            + state(new_p, batch, seq) + state(new_s, ns, 1))
```

```python
import functools
import math

import numpy as np
import jax
import jax.numpy as jnp
from jax import lax
from jax.experimental import pallas as pl
from jax.experimental.pallas import tpu as pltpu

F32 = jnp.float32
BF16 = jnp.bfloat16
I32 = jnp.int32
HIGHEST = lax.Precision.HIGHEST

HEAD_DIM = 128
PAGE_SIZE = 128
MOBA_BLOCK = 256
MOBA_TOPK = 3
DSA_TOPK = 256
IDX_DIM = 64
N_BUCKETS = 32
REL_MAX_DIST = 4096
PEER_NKEYS = 128
PEER_TOPK = 16
RMS_EPS = 1e-6
NEG = -1e30
LOG2E = math.log2(math.e)
LANES = 128
ATT_TILE = 256
VMEM_CAP = 56 * 1024 * 1024
HEADS_PER_STEP = 2
SCAN_BUFS = 8


def _bucket_thresholds():
    d = np.arange(0, 4 * REL_MAX_DIST + 2)
    max_exact = N_BUCKETS // 2
    nf = np.maximum(d, 1).astype(np.float32)
    large = max_exact + (np.log(nf / np.float32(max_exact)) / np.float32(math.log(REL_MAX_DIST / max_exact))
                         * np.float32(N_BUCKETS - max_exact)).astype(np.int32)
    large = np.minimum(large, N_BUCKETS - 1)
    bucket = np.where(d < max_exact, d, large)
    assert np.all(np.diff(bucket) >= 0) and bucket[-1] == N_BUCKETS - 1
    return [int(np.argmax(bucket >= b)) for b in range(1, N_BUCKETS)]


BUCKET_THR = _bucket_thresholds()


def _cparams(sem, vmem_bytes=None):
    kw = dict(dimension_semantics=sem)
    if vmem_bytes is not None:
        kw["vmem_limit_bytes"] = int(min(max(vmem_bytes, 16 * 1024 * 1024), VMEM_CAP))
    return pltpu.CompilerParams(**kw)


def _bias_from_dist(dist, bias_at):
    val = jnp.full(dist.shape, bias_at(0), F32)
    for b in range(1, N_BUCKETS):
        val = jnp.where(dist >= BUCKET_THR[b - 1], bias_at(b), val)
    return val


def _dot_nt(a, b, precision=None):
    return lax.dot_general(a, b, (((1,), (1,)), ((), ())), precision=precision, preferred_element_type=F32)


def _ada_kernel(c_ref, w_ref, b_ref, o_ref):
    c = c_ref[...]
    s = c * jax.nn.sigmoid(c)
    o_ref[...] = jnp.dot(s, w_ref[...], precision=HIGHEST, preferred_element_type=F32) + b_ref[...]


def _ada_mod(c, w_ada, b_ada):
    n, d = c.shape
    n6 = w_ada.shape[1]
    tn = 512
    return pl.pallas_call(
        _ada_kernel,
        grid=(n6 // tn,),
        in_specs=[pl.BlockSpec((n, d), lambda j: (0, 0)),
                  pl.BlockSpec((d, tn), lambda j: (0, j)),
                  pl.BlockSpec((1, tn), lambda j: (0, j))],
        out_specs=pl.BlockSpec((n, tn), lambda j: (0, j)),
        out_shape=jax.ShapeDtypeStruct((n, n6), F32),
        compiler_params=_cparams(("arbitrary",), 4 * d * tn * 4),
    )(c, w_ada, b_ada.reshape(1, n6))


def _prenorm_kernel(x_ref, g_ref, sc_ref, sh_ref, o_ref):
    x = x_ref[...]
    y = x * lax.rsqrt(jnp.mean(x * x, axis=-1, keepdims=True) + RMS_EPS) * g_ref[...]
    o_ref[...] = (y * (1.0 + sc_ref[0]) + sh_ref[0]).astype(o_ref.dtype)


def _mod_spec(mod, tm, tiles_per_group):
    g, r, d = mod.shape
    if r == 1:
        return pl.BlockSpec((1, 1, d), lambda i, *_: (i // tiles_per_group, 0, 0))
    return pl.BlockSpec((1, tm, d), lambda i, *_: (0, i, 0))


def _prenorm(x, gain, sc, sh, tm, tiles_per_group):
    r, d = x.shape
    return pl.pallas_call(
        _prenorm_kernel,
        grid=(r // tm,),
        in_specs=[pl.BlockSpec((tm, d), lambda i: (i, 0)),
                  pl.BlockSpec((1, d), lambda i: (0, 0)),
                  _mod_spec(sc, tm, tiles_per_group), _mod_spec(sh, tm, tiles_per_group)],
        out_specs=pl.BlockSpec((tm, d), lambda i: (i, 0)),
        out_shape=jax.ShapeDtypeStruct((r, d), BF16),
        compiler_params=_cparams(("parallel",), 8 * tm * d * 4),
    )(x, gain.reshape(1, d), sc, sh)


def _linear_kernel(x_ref, w_ref, *refs, headnorm):
    acc = jnp.dot(x_ref[...], w_ref[...], preferred_element_type=F32)
    if headnorm:
        g_ref, o_ref = refs
        for h in range(acc.shape[1] // HEAD_DIM):
            blk = acc[:, h * HEAD_DIM:(h + 1) * HEAD_DIM]
            ms = jnp.mean(blk * blk, axis=-1, keepdims=True)
            o_ref[:, h * HEAD_DIM:(h + 1) * HEAD_DIM] = blk * lax.rsqrt(ms + RMS_EPS) * g_ref[...]
    else:
        (o_ref,) = refs
        o_ref[...] = acc


def _linear(x, w, tm, head_gain=None):
    r, k = x.shape
    n = w.shape[1]
    in_specs = [pl.BlockSpec((tm, k), lambda i: (i, 0)), pl.BlockSpec((k, n), lambda i: (0, 0))]
    args = [x, w]
    if head_gain is not None:
        in_specs.append(pl.BlockSpec((1, HEAD_DIM), lambda i: (0, 0)))
        args.append(head_gain.reshape(1, HEAD_DIM))
    return pl.pallas_call(
        functools.partial(_linear_kernel, headnorm=head_gain is not None),
        grid=(r // tm,),
        in_specs=in_specs,
        out_specs=pl.BlockSpec((tm, n), lambda i: (i, 0)),
        out_shape=jax.ShapeDtypeStruct((r, n), F32),
        compiler_params=_cparams(("parallel",), 2 * (tm * k * 2 + k * n * 2 + tm * n * 4) + 4 * tm * n * 4),
    )(*args)


def _linear_t_kernel(w_ref, x_ref, o_ref):
    o_ref[...] = _dot_nt(w_ref[...], x_ref[...])


def _linear_t(w_t, x, tm):
    r, k = x.shape
    n = w_t.shape[0]
    return pl.pallas_call(
        _linear_t_kernel,
        grid=(r // tm,),
        in_specs=[pl.BlockSpec((n, k), lambda i: (0, 0)), pl.BlockSpec((tm, k), lambda i: (i, 0))],
        out_specs=pl.BlockSpec((n, tm), lambda i: (0, i)),
        out_shape=jax.ShapeDtypeStruct((n, r), F32),
        compiler_params=_cparams(("parallel",)),
    )(w_t, x)


def _bias_table_kernel(bias_ref, o_ref):
    h = pl.program_id(0)
    shape = o_ref.shape[1:]
    dist = lax.broadcasted_iota(I32, shape, 1) - lax.broadcasted_iota(I32, shape, 0)
    o_ref[0] = jnp.where(dist >= 0, _bias_from_dist(dist, lambda b: bias_ref[h, b] * LOG2E), NEG)


def _bias_tables(rel_bias_t, seq):
    nh = rel_bias_t.shape[0]
    return pl.pallas_call(
        _bias_table_kernel,
        grid=(nh,),
        in_specs=[pl.BlockSpec(memory_space=pltpu.SMEM)],
        out_specs=pl.BlockSpec((1, ATT_TILE, seq), lambda h: (h, 0, 0)),
        out_shape=jax.ShapeDtypeStruct((nh, ATT_TILE, seq), F32),
        compiler_params=_cparams(("parallel",), 4 * ATT_TILE * seq * 4),
    )(rel_bias_t)


def _flash_attend(n_trips, block_of, mask_of, q_ref, kbf_ref, vt_ref, tb_ref, o_ref, s_ref, p_ref, acc_ref):
    qt = pl.program_id(2)
    heads = _head_slices()
    qs = [(q_ref[:, hs] * (HEAD_DIM ** -0.5 * LOG2E)).astype(BF16) for hs in heads]

    def key_rows(blk):
        return pl.ds(pl.multiple_of(blk * ATT_TILE, ATT_TILE), ATT_TILE)

    first = block_of(0)
    for g, hs in enumerate(heads):
        s_ref[g] = _dot_nt(kbf_ref[key_rows(first), hs], qs[g])
        p_ref[g] = jnp.zeros(p_ref.shape[1:], p_ref.dtype)
        acc_ref[g] = jnp.zeros(acc_ref.shape[1:], acc_ref.dtype)

    def body(i, carries):
        blk = block_of(i)
        prev_rows = key_rows(block_of(jnp.maximum(i - 1, 0)))
        next_rows = key_rows(block_of(jnp.minimum(i + 1, n_trips - 1)))
        delta = pl.multiple_of((qt - blk) * ATT_TILE, ATT_TILE)
        out = []
        for g, hs in enumerate(heads):
            m, l = carries[g]
            s = s_ref[g] + tb_ref[g, :, pl.ds(delta, ATT_TILE)] + mask_of(blk, g)
            m_new = jnp.maximum(m, jnp.max(s, axis=0, keepdims=True))
            alpha = jnp.exp2(m - m_new)
            p = jnp.exp2(s - m_new)
            l = alpha * l + jnp.sum(p, axis=0, keepdims=True)
            pv = jnp.dot(vt_ref[hs, prev_rows], p_ref[g], preferred_element_type=F32)
            acc_ref[g] = (acc_ref[g] + pv) * alpha
            p_ref[g] = p.astype(BF16)
            s_ref[g] = _dot_nt(kbf_ref[next_rows, hs], qs[g])
            out.append((m_new, l))
        return tuple(out)

    m0 = jnp.full((1, ATT_TILE), NEG, F32)
    l0 = jnp.zeros((1, ATT_TILE), F32)
    carries = lax.fori_loop(0, n_trips, body, tuple((m0, l0) for _ in heads))
    last_rows = key_rows(block_of(n_trips - 1))
    for g, hs in enumerate(heads):
        _, l = carries[g]
        acc = acc_ref[g] + jnp.dot(vt_ref[hs, last_rows], p_ref[g], preferred_element_type=F32)
        o_ref[:, hs] = (acc / l).T


def _top_rows(s, k):
    n = s.shape[0]
    rows = lax.broadcasted_iota(I32, s.shape, 0)
    vals, picks = [], []
    for _ in range(k):
        m = jnp.max(s, axis=0, keepdims=True)
        first = jnp.min(jnp.where(s == m, rows, n), axis=0, keepdims=True)
        pick = rows == first
        s = jnp.where(pick, -jnp.inf, s)
        vals.append(m)
        picks.append(pick)
    return vals, picks


def _head_slices():
    return [slice(g * HEAD_DIM, (g + 1) * HEAD_DIM) for g in range(HEADS_PER_STEP)]


def _load_kv_scratch(k_ref, v_ref, kbf_ref, vt_ref):
    n_blk = k_ref.shape[0] // ATT_TILE
    for b in range(n_blk):
        sl = slice(b * ATT_TILE, (b + 1) * ATT_TILE)
        kbf_ref[sl, :] = k_ref[sl, :].astype(BF16)
        vt_ref[:, sl] = v_ref[sl, :].T.astype(BF16)


def _moba_kernel(q_ref, k_ref, v_ref, tb_ref, o_ref, kbf_ref, vt_ref, s_ref, p_ref, acc_ref, kmean_ref,
                 selb_ref):
    qt = pl.program_id(2)
    n_blk = k_ref.shape[0] // ATT_TILE
    heads = _head_slices()

    @pl.when(qt == 0)
    def _():
        _load_kv_scratch(k_ref, v_ref, kbf_ref, vt_ref)
        for b in range(n_blk):
            kmean_ref[b:b + 1, :] = jnp.mean(k_ref[b * ATT_TILE:(b + 1) * ATT_TILE, :], axis=0, keepdims=True)

    for g, hs in enumerate(heads):
        gate = _dot_nt(kmean_ref[:, hs], q_ref[:, hs], precision=HIGHEST)
        blocks = lax.broadcasted_iota(I32, gate.shape, 0)
        past = blocks < qt
        _, picks = _top_rows(jnp.where(past, gate, NEG), min(MOBA_TOPK, n_blk))
        sel = blocks == qt
        for p in picks:
            sel = sel | (p & past)
        selb_ref[g] = jnp.where(sel, 0.0, NEG)

    _flash_attend(qt + 1, lambda i: jnp.where(i == 0, qt, i - 1),
                  lambda blk, g: selb_ref[g, pl.ds(blk, 1), :],
                  q_ref, kbf_ref, vt_ref, tb_ref, o_ref, s_ref, p_ref, acc_ref)


def _attn_specs(seq, nq, table_off):
    wide = HEADS_PER_STEP * HEAD_DIM
    tile = pl.BlockSpec((ATT_TILE, wide), lambda b, h, t: (b * nq + t, h))
    full = pl.BlockSpec((seq, wide), lambda b, h, t: (b, h))
    table = pl.BlockSpec((HEADS_PER_STEP, ATT_TILE, seq), lambda b, h, t: (h + table_off, 0, 0))
    scratch = [pltpu.VMEM((seq, wide), BF16), pltpu.VMEM((wide, seq), BF16),
               pltpu.VMEM((HEADS_PER_STEP, ATT_TILE, ATT_TILE), F32),
               pltpu.VMEM((HEADS_PER_STEP, ATT_TILE, ATT_TILE), BF16),
               pltpu.VMEM((HEADS_PER_STEP, HEAD_DIM, ATT_TILE), F32)]
    vmem = 2 * (2 * seq * wide * 4 + HEADS_PER_STEP * ATT_TILE * seq * 4) + 2 * seq * wide * 2
    return tile, full, table, scratch, vmem


def _moba_prompt(q, k, v, tables, batch, seq, n_heads):
    nq = seq // ATT_TILE
    n_blk = seq // MOBA_BLOCK
    tile, full, table, scratch, vmem = _attn_specs(seq, nq, 0)
    return pl.pallas_call(
        _moba_kernel,
        grid=(batch, n_heads // HEADS_PER_STEP, nq),
        in_specs=[tile, full, full, table],
        out_specs=tile,
        out_shape=jax.ShapeDtypeStruct(q.shape, F32),
        scratch_shapes=scratch + [pltpu.VMEM((n_blk, HEADS_PER_STEP * HEAD_DIM), F32),
                                  pltpu.VMEM((HEADS_PER_STEP, n_blk, ATT_TILE), F32)],
        compiler_params=_cparams(("parallel", "parallel", "arbitrary"), vmem + 8 * 1024 * 1024),
    )(q, k, v, tables)


def _sortable(x):
    bits = lax.bitcast_convert_type(x, I32)
    return bits ^ ((bits >> 31) & 0x7FFFFFFF)


def _dsa_index_kernel(qi_ref, wt_ref, ki_ref, o_ref, u_ref, *, n_sel, n_idx_heads):
    qt = pl.program_id(1)
    nq = ki_ref.shape[0] // ATT_TILE
    w = wt_ref[...] * (IDX_DIM ** -0.5 * n_idx_heads ** -0.5)
    qi = qi_ref[...].astype(BF16)
    rows = lax.broadcasted_iota(I32, (ATT_TILE, ATT_TILE), 0)
    cols = lax.broadcasted_iota(I32, (ATT_TILE, ATT_TILE), 1)

    def score_body(kb, _):
        kk = pl.multiple_of(kb * ATT_TILE, ATT_TILE)
        kblk = ki_ref[pl.ds(kk, ATT_TILE), :].astype(BF16)
        sc = jnp.zeros((ATT_TILE, ATT_TILE), F32)
        for h in range(n_idx_heads):
            sh = _dot_nt(kblk, qi[:, h * IDX_DIM:(h + 1) * IDX_DIM])
            sc = sc + jnp.maximum(sh, 0.0) * w[h:h + 1, :]
        sc = jnp.where((kb < qt) | (rows <= cols), sc, NEG)
        u_ref[pl.ds(kk, ATT_TILE), :] = _sortable(sc)
        return 0

    lax.fori_loop(0, qt + 1, score_body, 0)

    def bit_body(i, t):
        cand = t + lax.shift_left(jnp.int32(1), 31 - i)

        def count_body(kb, cnt):
            kk = pl.multiple_of(kb * ATT_TILE, ATT_TILE)
            hit = (u_ref[pl.ds(kk, ATT_TILE), :] >= cand).astype(I32)
            return cnt + jnp.sum(hit, axis=0, keepdims=True)

        cnt = lax.fori_loop(0, qt + 1, count_body, jnp.zeros((1, ATT_TILE), I32))
        return jnp.where(cnt >= n_sel, cand, t)

    thr = lax.fori_loop(0, 32, bit_body, jnp.full((1, ATT_TILE), -2 ** 31, I32))

    def out_body(kb, _):
        kk = pl.multiple_of(kb * ATT_TILE, ATT_TILE)
        keep = (u_ref[pl.ds(kk, ATT_TILE), :] >= thr) & ((kb < qt) | (rows <= cols))
        o_ref[0, pl.ds(kk, ATT_TILE), :] = jnp.where(keep, 0.0, NEG).astype(o_ref.dtype)
        return 0

    def future_body(kb, _):
        kk = pl.multiple_of(kb * ATT_TILE, ATT_TILE)
        o_ref[0, pl.ds(kk, ATT_TILE), :] = jnp.full((ATT_TILE, ATT_TILE), NEG, o_ref.dtype)
        return 0

    lax.fori_loop(0, qt + 1, out_body, 0)
    lax.fori_loop(qt + 1, nq, future_body, 0)


def _dsa_index(qi, wi_t, ki, batch, seq, n_idx_heads):
    nq = seq // ATT_TILE
    n_sel = min(DSA_TOPK, seq // 4)
    return pl.pallas_call(
        functools.partial(_dsa_index_kernel, n_sel=n_sel, n_idx_heads=n_idx_heads),
        grid=(batch, nq),
        in_specs=[pl.BlockSpec((ATT_TILE, qi.shape[1]), lambda b, t: (b * nq + t, 0)),
                  pl.BlockSpec((n_idx_heads, ATT_TILE), lambda b, t: (0, b * nq + t)),
                  pl.BlockSpec((seq, IDX_DIM), lambda b, t: (b, 0))],
        out_specs=pl.BlockSpec((1, seq, ATT_TILE), lambda b, t: (b, 0, t)),
        out_shape=jax.ShapeDtypeStruct((batch, seq, seq), BF16),
        scratch_shapes=[pltpu.VMEM((seq, ATT_TILE), I32)],
        compiler_params=_cparams(("parallel", "arbitrary"), 32 * 1024 * 1024),
    )(qi, wi_t, ki)


def _dsa_attn_kernel(q_ref, k_ref, v_ref, tb_ref, mask_ref, o_ref, kbf_ref, vt_ref, s_ref, p_ref, acc_ref):
    qt = pl.program_id(2)

    @pl.when(qt == 0)
    def _():
        _load_kv_scratch(k_ref, v_ref, kbf_ref, vt_ref)

    def mask_of(blk, g):
        return mask_ref[0, pl.ds(pl.multiple_of(blk * ATT_TILE, ATT_TILE), ATT_TILE), :].astype(F32)

    _flash_attend(qt + 1, lambda i: i, mask_of, q_ref, kbf_ref, vt_ref, tb_ref, o_ref, s_ref, p_ref, acc_ref)


def _dsa_prompt(q, k, v, tables, mask, batch, seq, n_heads, head_off):
    nq = seq // ATT_TILE
    tile, full, table, scratch, vmem = _attn_specs(seq, nq, head_off // HEADS_PER_STEP)
    return pl.pallas_call(
        _dsa_attn_kernel,
        grid=(batch, n_heads // HEADS_PER_STEP, nq),
        in_specs=[tile, full, full, table, pl.BlockSpec((1, seq, ATT_TILE), lambda b, h, t: (b, 0, t))],
        out_specs=tile,
        out_shape=jax.ShapeDtypeStruct(q.shape, F32),
        scratch_shapes=scratch,
        compiler_params=_cparams(("parallel", "parallel", "arbitrary"),
                                 vmem + 2 * seq * ATT_TILE * 2 + 8 * 1024 * 1024),
    )(q, k, v, tables, mask)


def _merge_kernel(oa_ref, ob_ref, ga_ref, gb_ref, wa_ref, wb_ref, o_ref):
    bra = jnp.dot(oa_ref[...].astype(BF16), wa_ref[...], preferred_element_type=F32)
    brb = jnp.dot(ob_ref[...].astype(BF16), wb_ref[...], preferred_element_type=F32)
    o_ref[...] = (jax.nn.sigmoid(ga_ref[...]) * bra + jax.nn.sigmoid(gb_ref[...]) * brb).astype(o_ref.dtype)


def _merge(out_a, out_b, ga, gb, w_pa, w_pb, tm):
    r, w = out_a.shape
    d = ga.shape[1]
    row = lambda n: pl.BlockSpec((tm, n), lambda i: (i, 0))
    full = lambda a: pl.BlockSpec(a.shape, lambda i: (0, 0))
    return pl.pallas_call(
        _merge_kernel,
        grid=(r // tm,),
        in_specs=[row(w), row(w), row(d), row(d), full(w_pa), full(w_pb)],
        out_specs=row(d),
        out_shape=jax.ShapeDtypeStruct((r, d), BF16),
        compiler_params=_cparams(("parallel",), 4 * w * d * 2 + 8 * tm * d * 4),
    )(out_a, out_b, ga, gb, w_pa, w_pb)


def _outproj_kernel(mg_ref, w_ref, x_ref, g1_ref, ng_ref, sc_ref, sh_ref, h_ref, hm_ref):
    h = x_ref[...] + g1_ref[0] * jnp.dot(mg_ref[...], w_ref[...], preferred_element_type=F32)
    h_ref[...] = h
    y = h * lax.rsqrt(jnp.mean(h * h, axis=-1, keepdims=True) + RMS_EPS) * ng_ref[...]
    hm_ref[...] = (y * (1.0 + sc_ref[0]) + sh_ref[0]).astype(hm_ref.dtype)


def _outproj(merged, w_out, x, g1, norm_g, sc2, sh2, tm, tiles_per_group):
    r, d = x.shape
    row = pl.BlockSpec((tm, d), lambda i: (i, 0))
    ms = lambda m: _mod_spec(m, tm, tiles_per_group)
    return pl.pallas_call(
        _outproj_kernel,
        grid=(r // tm,),
        in_specs=[row, pl.BlockSpec((d, d), lambda i: (0, 0)), row, ms(g1),
                  pl.BlockSpec((1, d), lambda i: (0, 0)), ms(sc2), ms(sh2)],
        out_specs=[row, row],
        out_shape=[jax.ShapeDtypeStruct((r, d), F32), jax.ShapeDtypeStruct((r, d), BF16)],
        compiler_params=_cparams(("parallel",), 4 * d * d * 2 + 12 * tm * d * 4),
    )(merged, w_out, x, g1, norm_g.reshape(1, d), sc2, sh2)


def _peer_pairs(k):
    return [(i, j) for i in range(k) for j in range(k) if (i + 1) * (j + 1) <= k]


def _peer_route_kernel(q_ref, k1_ref, k2_ref, e1_ref, e2_ref, tau_ref):
    n_heads = k1_ref.shape[0]
    dk = k1_ref.shape[2]
    kk = PEER_TOPK + 1
    for h in range(n_heads):
        q1 = q_ref[:, (2 * h) * dk:(2 * h + 1) * dk]
        q2 = q_ref[:, (2 * h + 1) * dk:(2 * h + 2) * dk]
        s1 = _dot_nt(k1_ref[h], q1, precision=HIGHEST)
        s2 = _dot_nt(k2_ref[h], q2, precision=HIGHEST)
        v1, _ = _top_rows(s1, kk)
        v2, _ = _top_rows(s2, kk)
        cand = jnp.concatenate([v1[i] + v2[j] for i, j in _peer_pairs(kk)], axis=0)
        c, _ = _top_rows(cand, kk)
        top = c[0]
        z = sum(jnp.exp(ck - top) for ck in c[:PEER_TOPK])
        inv_z = 1.0 / z
        e1_ref[h] = jnp.exp(s1 - v1[0]) * inv_z
        e2_ref[h] = jnp.exp(s2 - v2[0])
        tau_ref[h:h + 1, :] = jnp.exp(0.5 * (c[PEER_TOPK - 1] + c[PEER_TOPK]) - top) * inv_z


def _peer_route(qp, keys1, keys2, tm):
    r, d = qp.shape
    nh, nk, dk = keys1.shape
    return pl.pallas_call(
        _peer_route_kernel,
        grid=(r // tm,),
        in_specs=[pl.BlockSpec((tm, d), lambda i: (i, 0)),
                  pl.BlockSpec((nh, nk, dk), lambda i: (0, 0, 0)),
                  pl.BlockSpec((nh, nk, dk), lambda i: (0, 0, 0))],
        out_specs=[pl.BlockSpec((nh, nk, tm), lambda i: (0, 0, i)),
                   pl.BlockSpec((nh, nk, tm), lambda i: (0, 0, i)),
                   pl.BlockSpec((nh, tm), lambda i: (0, i))],
        out_shape=[jax.ShapeDtypeStruct((nh, nk, r), F32), jax.ShapeDtypeStruct((nh, nk, r), F32),
                   jax.ShapeDtypeStruct((nh, r), F32)],
        compiler_params=_cparams(("parallel",), 32 * 1024 * 1024),
    )(qp, keys1, keys2)


def _gelu_tanh(x):
    return 0.5 * x * (1.0 + jnp.tanh(math.sqrt(2.0 / math.pi) * (x + 0.044715 * (x * x * x))))


def _peer_dense_kernel(hm_ref, u_ref, v_ref, e1_ref, e2_ref, tau_ref, o_ref, act_a, act_b):
    j = pl.program_id(1)
    n_tiles = pl.num_programs(1) - 1
    te = u_ref.shape[0]
    n_heads = e1_ref.shape[0]
    rows_per_tile = te // PEER_NKEYS

    d = hm_ref.shape[1]
    rows_per_dot = min(rows_per_tile, 2)
    k_chunk = d // rows_per_tile

    def step(dst, src):
        proj, total, pieces = None, None, []
        for r in range(rows_per_tile):
            if dst is not None:
                ks = slice(r * k_chunk, (r + 1) * k_chunk)
                part = _dot_nt(u_ref[:, ks], hm_ref[:, ks])
                proj = part if proj is None else proj + part
            if src is not None:
                i1 = (j - 1) * rows_per_tile + r
                g = jnp.zeros((PEER_NKEYS, src.shape[1]), F32)
                for h in range(n_heads):
                    w = e2_ref[h] * e1_ref[h, pl.ds(i1, 1), :]
                    g = g + jnp.where(w >= tau_ref[h:h + 1, :], w, 0.0)
                a_t = g * _gelu_tanh(src[r * PEER_NKEYS:(r + 1) * PEER_NKEYS, :])
                pieces.append(a_t.T.astype(BF16))
                if len(pieces) == rows_per_dot:
                    a = jnp.concatenate(pieces, axis=1) if rows_per_dot > 1 else pieces[0]
                    r0 = r + 1 - rows_per_dot
                    part = jnp.dot(a, v_ref[r0 * PEER_NKEYS:(r + 1) * PEER_NKEYS, :], preferred_element_type=F32)
                    total = part if total is None else total + part
                    pieces = []
        if dst is not None:
            dst[...] = proj
        if src is not None:
            o_ref[...] += total

    @pl.when(j == 0)
    def _():
        o_ref[...] = jnp.zeros_like(o_ref)
        step(act_a, None)

    @pl.when((j > 0) & (j < n_tiles) & (j % 2 == 0))
    def _():
        step(act_a, act_b)

    @pl.when((j > 0) & (j < n_tiles) & (j % 2 == 1))
    def _():
        step(act_b, act_a)

    @pl.when((j == n_tiles) & (j % 2 == 0))
    def _():
        step(None, act_b)

    @pl.when((j == n_tiles) & (j % 2 == 1))
    def _():
        step(None, act_a)


def _peer_dense(hm, u, v, e1n, e2, tau, tm, te):
    r, d = hm.shape
    n_tiles = u.shape[0] // te
    nh, nk, _ = e1n.shape
    return pl.pallas_call(
        _peer_dense_kernel,
        grid=(r // tm, n_tiles + 1),
        in_specs=[pl.BlockSpec((tm, d), lambda i, j: (i, 0)),
                  pl.BlockSpec((te, d), lambda i, j: (jnp.minimum(j, n_tiles - 1), 0)),
                  pl.BlockSpec((te, d), lambda i, j: (jnp.maximum(j - 1, 0), 0)),
                  pl.BlockSpec((nh, nk, tm), lambda i, j: (0, 0, i)),
                  pl.BlockSpec((nh, nk, tm), lambda i, j: (0, 0, i)),
                  pl.BlockSpec((nh, tm), lambda i, j: (0, i))],
        out_specs=pl.BlockSpec((tm, d), lambda i, j: (i, 0)),
        out_shape=jax.ShapeDtypeStruct((r, d), F32),
        scratch_shapes=[pltpu.VMEM((te, tm), F32), pltpu.VMEM((te, tm), F32)],
        compiler_params=_cparams(("parallel", "arbitrary"),
                                 2 * (tm * d * 2 + 2 * te * d * 2 + 2 * nh * nk * tm * 4 + tm * d * 4)
                                 + 10 * te * tm * 4),
    )(hm, u, v, e1n, e2, tau)


def _residual_kernel(h_ref, p_ref, g2_ref, o_ref):
    o_ref[...] = h_ref[...] + g2_ref[0] * p_ref[...]


def _residual(h, p, g2, tm, tiles_per_group):
    r, d = h.shape
    row = pl.BlockSpec((tm, d), lambda i: (i, 0))
    return pl.pallas_call(
        _residual_kernel,
        grid=(r // tm,),
        in_specs=[row, row, _mod_spec(g2, tm, tiles_per_group)],
        out_specs=row,
        out_shape=jax.ShapeDtypeStruct((r, d), F32),
        compiler_params=_cparams(("parallel",), 8 * tm * d * 4),
    )(h, p, g2)


def _sample_scan_kernel(pt_ref, qa_ref, qi_ref, wi_ref, kinew_ref, ka_hbm, kidx_hbm, blk_ref, score_ref,
                        kbuf, ibuf, ksum_ref, ksem, isem, *, n_idx_heads, n_heads):
    ns, n_pages = pt_ref.shape
    total = ns * n_pages
    pages_per_blk = MOBA_BLOCK // PAGE_SIZE
    n_blk = ksum_ref.shape[0]
    n_bufs = kbuf.shape[0]

    def page_copies(g, slot):
        s = g // n_pages
        page = pt_ref[s, g - s * n_pages]
        return (pltpu.make_async_copy(ka_hbm.at[0, page], kbuf.at[slot], ksem.at[slot]),
                pltpu.make_async_copy(kidx_hbm.at[0, page], ibuf.at[slot], isem.at[slot]))

    for g in range(n_bufs):
        for c in page_copies(g, g):
            c.start()

    def body(g, _):
        slot = g % n_bufs
        s = g // n_pages
        p = g - s * n_pages
        for c in page_copies(g, slot):
            c.wait()

        @pl.when(p == 0)
        def _():
            ksum_ref[...] = jnp.zeros_like(ksum_ref)
            score_ref[s, pl.ds(n_pages, 8), :] = jnp.full((8, PAGE_SIZE), -jnp.inf, F32)

        ksum_ref[p // pages_per_blk] += jnp.sum(kbuf[slot], axis=0)

        w = wi_ref[s] * (IDX_DIM ** -0.5 * n_idx_heads ** -0.5)
        qi = qi_ref[s]
        sc = jnp.maximum(_dot_nt(qi.astype(BF16), ibuf[slot].astype(BF16)), 0.0) * w
        score_ref[s, pl.ds(p, 1), :] = jnp.sum(sc, axis=0, keepdims=True)

        @pl.when(g + n_bufs < total)
        def _():
            for c in page_copies(g + n_bufs, slot):
                c.start()

        @pl.when(p == n_pages - 1)
        def _():
            kn = kinew_ref[s].astype(BF16).astype(F32)
            dots = jnp.sum(qi.astype(BF16).astype(F32) * kn, axis=-1, keepdims=True)
            new_score = jnp.sum(jnp.maximum(dots, 0.0) * w, axis=0, keepdims=True)
            lane = lax.broadcasted_iota(I32, (1, PAGE_SIZE), 1)
            score_ref[s, pl.ds(n_pages, 1), :] = jnp.where(lane == 0, new_score, -jnp.inf)

            gate = jnp.sum(ksum_ref[...] * (qa_ref[s] * (1.0 / MOBA_BLOCK)), axis=-1)
            rows = lax.broadcasted_iota(I32, gate.shape, 0)
            out = jnp.zeros(blk_ref.shape[1:], I32)
            orow = lax.broadcasted_iota(I32, out.shape, 0)
            for t in range(MOBA_TOPK):
                m = jnp.max(gate, axis=0, keepdims=True)
                first = jnp.min(jnp.where(gate == m, rows, n_blk), axis=0, keepdims=True)
                gate = jnp.where(rows == first, -jnp.inf, gate)
                padded = jnp.concatenate([first, jnp.zeros((1, out.shape[1] - n_heads), I32)], axis=1)
                out = jnp.where(orow == t, padded, out)
            blk_ref[s] = out
        return 0

    lax.fori_loop(0, total, body, 0)


def _sample_scan(page_table, cache_ka, cache_kidx, qa, qi, wi, ki_new, n_heads, n_idx_heads):
    ns, n_pages = page_table.shape
    n_blk = n_pages * PAGE_SIZE // MOBA_BLOCK
    n_bufs = min(SCAN_BUFS, ns * n_pages)
    whole = lambda a: pl.BlockSpec(a.shape, lambda i, pt: (0,) * a.ndim)
    hbm = pl.BlockSpec(memory_space=pl.ANY)
    out_shape = [jax.ShapeDtypeStruct((ns, 8, LANES), I32),
                 jax.ShapeDtypeStruct((ns, n_pages + 8, PAGE_SIZE), F32)]
    grid_spec = pltpu.PrefetchScalarGridSpec(
        num_scalar_prefetch=1,
        grid=(1,),
        in_specs=[whole(qa), whole(qi), whole(wi), whole(ki_new), hbm, hbm],
        out_specs=[whole(o) for o in out_shape],
        scratch_shapes=[pltpu.VMEM((n_bufs, PAGE_SIZE, n_heads, HEAD_DIM), F32),
                        pltpu.VMEM((n_bufs, PAGE_SIZE, IDX_DIM), F32),
                        pltpu.VMEM((n_blk, n_heads, HEAD_DIM), F32),
                        pltpu.SemaphoreType.DMA((n_bufs,)), pltpu.SemaphoreType.DMA((n_bufs,))])
    return pl.pallas_call(
        functools.partial(_sample_scan_kernel, n_idx_heads=n_idx_heads, n_heads=n_heads),
        grid_spec=grid_spec,
        out_shape=out_shape,
        compiler_params=_cparams(("arbitrary",), 24 * 1024 * 1024),
    )(page_table, qa, qi, wi, ki_new, cache_ka, cache_kidx)


def _sample_topk_kernel(score_ref, idx_ref, *, n_sel, n_valid):
    s = score_ref[...]
    n = s.shape[1]
    lane = lax.broadcasted_iota(I32, s.shape, 1)
    s = jnp.where(lane < n_valid, s, -jnp.inf)
    out_lane = lax.broadcasted_iota(I32, idx_ref.shape, 1)

    def body(t, carry):
        s, out = carry
        m = jnp.max(s, axis=1, keepdims=True)
        first = jnp.min(jnp.where(s == m, lane, n), axis=1, keepdims=True)
        s = jnp.where(lane == first, -jnp.inf, s)
        return s, jnp.where(out_lane == t, first, out)

    _, out = lax.fori_loop(0, n_sel, body, (s, jnp.zeros(idx_ref.shape, I32)))
    idx_ref[...] = out


def _sample_topk(score, n_sel, n_valid):
    ns, n = score.shape
    return pl.pallas_call(
        functools.partial(_sample_topk_kernel, n_sel=n_sel, n_valid=n_valid),
        out_shape=jax.ShapeDtypeStruct((ns, n_sel), I32),
        compiler_params=_cparams(None, 8 * ns * n * 4),
    )(score)


def _sample_moba_kernel(pt_ref, blk_ref, bias_ref, q_ref, kn_ref, vn_ref, ck_hbm, cv_hbm, o_ref,
                        kbuf, vbuf, sem, *, past, n_heads):
    s_id = pl.program_id(0)
    pages_per_blk = MOBA_BLOCK // PAGE_SIZE
    slabs = [(h, t, half) for h in range(n_heads) for t in range(MOBA_TOPK) for half in range(pages_per_blk)]

    def slab_copies(i):
        h, t, half = slabs[i]
        page = pt_ref[s_id, blk_ref[s_id, h * MOBA_TOPK + t] * pages_per_blk + half]
        return (pltpu.make_async_copy(ck_hbm.at[0, page, pl.ds(0, PAGE_SIZE), h], kbuf.at[i], sem.at[0]),
                pltpu.make_async_copy(cv_hbm.at[0, page, pl.ds(0, PAGE_SIZE), h], vbuf.at[i], sem.at[1]))

    for i in range(len(slabs)):
        for c in slab_copies(i):
            c.start()
    for i in range(len(slabs)):
        for c in slab_copies(i):
            c.wait()

    lane = lax.broadcasted_iota(I32, (1, PAGE_SIZE), 1)
    outs = []
    for h in range(n_heads):
        q8 = jnp.broadcast_to(q_ref[0, h:h + 1, :] * HEAD_DIM ** -0.5, (8, HEAD_DIM)).astype(BF16)
        kn = kn_ref[0, h:h + 1, :].astype(BF16).astype(F32)
        s_own = jnp.sum(q8[0:1].astype(F32) * kn, axis=-1, keepdims=True) + bias_ref[h, 0]
        logits = []
        m = s_own
        for i, (hh, t, half) in enumerate(slabs):
            if hh != h:
                continue
            key0 = blk_ref[s_id, h * MOBA_TOPK + t] * MOBA_BLOCK + half * PAGE_SIZE
            dist = past - (key0 + lane)
            s = _dot_nt(q8, kbuf[i].astype(BF16))[0:1] + _bias_from_dist(dist, lambda b: bias_ref[h, b])
            s = jnp.where(dist >= 0, s, NEG)
            logits.append((i, s))
            m = jnp.maximum(m, jnp.max(s, axis=-1, keepdims=True))
        p_own = jnp.exp(s_own - m)
        l = p_own
        acc = p_own * vn_ref[0, h:h + 1, :].astype(BF16).astype(F32)
        for i, s in logits:
            p = jnp.exp(s - m)
            l = l + jnp.sum(p, axis=-1, keepdims=True)
            p8 = jnp.broadcast_to(p, (8, PAGE_SIZE)).astype(BF16)
            acc = acc + jnp.dot(p8, vbuf[i].astype(BF16), preferred_element_type=F32)[0:1]
        outs.append(acc / l)
    o_ref[0] = jnp.concatenate(outs, axis=0)


def _sample_moba(page_table, blk_flat, rel_bias_t, qa, cache_ka, cache_va, ka_new, va_new, n_heads):
    ns, n_pages = page_table.shape
    n_slabs = n_heads * MOBA_TOPK * (MOBA_BLOCK // PAGE_SIZE)
    heads = pl.BlockSpec((1, n_heads, HEAD_DIM), lambda s, pt, blk: (s, 0, 0))
    hbm = pl.BlockSpec(memory_space=pl.ANY)
    grid_spec = pltpu.PrefetchScalarGridSpec(
        num_scalar_prefetch=2,
        grid=(ns,),
        in_specs=[pl.BlockSpec(memory_space=pltpu.SMEM), heads, heads, heads, hbm, hbm],
        out_specs=heads,
        scratch_shapes=[pltpu.VMEM((n_slabs, PAGE_SIZE, HEAD_DIM), F32),
                        pltpu.VMEM((n_slabs, PAGE_SIZE, HEAD_DIM), F32),
                        pltpu.SemaphoreType.DMA((2,))])
    return pl.pallas_call(
        functools.partial(_sample_moba_kernel, past=n_pages * PAGE_SIZE, n_heads=n_heads),
        grid_spec=grid_spec,
        out_shape=jax.ShapeDtypeStruct(qa.shape, F32),
        compiler_params=_cparams(("arbitrary",), 24 * 1024 * 1024),
    )(page_table, blk_flat, rel_bias_t, qa, ka_new, va_new, cache_ka, cache_va)


def _sample_dsa_kernel(pt_ref, sel_ref, bias_ref, selv_ref, q_ref, kn_ref, vn_ref, ck_ref, cv_ref, o_ref,
                       kbuf, vbuf, sem, *, past, n_heads, head_off):
    s_id = pl.program_id(0)
    n_samples = pl.num_programs(0)
    n_sel = kbuf.shape[1]
    slot = s_id % 2

    def row_copies(sample, buf, r):
        idx = sel_ref[sample, r]
        page = pt_ref[sample, jnp.minimum(idx, past - 1) // PAGE_SIZE]
        pos = idx % PAGE_SIZE
        dst_k, dst_v = kbuf.at[buf, pl.ds(r, 1)], vbuf.at[buf, pl.ds(r, 1)]
        cached = (pltpu.make_async_copy(ck_ref.at[0, page, pl.ds(pos, 1)], dst_k, sem.at[buf, 0]),
                  pltpu.make_async_copy(cv_ref.at[0, page, pl.ds(pos, 1)], dst_v, sem.at[buf, 1]))
        fresh = (pltpu.make_async_copy(kn_ref.at[pl.ds(sample, 1)], dst_k, sem.at[buf, 0]),
                 pltpu.make_async_copy(vn_ref.at[pl.ds(sample, 1)], dst_v, sem.at[buf, 1]))
        return idx, cached, fresh

    def gather(sample, buf, action):
        def body(r, _):
            idx, cached, fresh = row_copies(sample, buf, r)

            @pl.when(idx < past)
            def _():
                for c in cached:
                    action(c)

            @pl.when(idx >= past)
            def _():
                for c in fresh:
                    action(c)
            return 0

        lax.fori_loop(0, n_sel, body, 0)

    @pl.when(s_id == 0)
    def _():
        gather(s_id, slot, lambda c: c.start())

    @pl.when(s_id + 1 < n_samples)
    def _():
        gather(s_id + 1, 1 - slot, lambda c: c.start())

    gather(s_id, slot, lambda c: c.wait())

    dist = past - selv_ref[0]
    outs = []
    for h in range(n_heads):
        q8 = jnp.broadcast_to(q_ref[0, h:h + 1, :] * HEAD_DIM ** -0.5, (8, HEAD_DIM)).astype(BF16)
        s = _dot_nt(q8, kbuf[slot, :, h, :].astype(BF16))[0:1]
        s = s + _bias_from_dist(dist, lambda b: bias_ref[h + head_off, b])
        s = jnp.where(dist >= 0, s, NEG)
        p = jnp.exp(s - jnp.max(s, axis=-1, keepdims=True))
        l = jnp.sum(p, axis=-1, keepdims=True)
        p8 = jnp.broadcast_to(p, (8, n_sel)).astype(BF16)
        outs.append(jnp.dot(p8, vbuf[slot, :, h, :].astype(BF16), preferred_element_type=F32)[0:1] / l)
    o_ref[0] = jnp.concatenate(outs, axis=0)


def _sample_dsa(page_table, sel, rel_bias_t, qb, kb_new, vb_new, cache_kb, cache_vb, n_heads, head_off):
    ns, n_pages = page_table.shape
    n_sel = sel.shape[1]
    row = pl.BlockSpec((1, n_heads, HEAD_DIM), lambda s, pt, sl: (s, 0, 0))
    hbm = pl.BlockSpec(memory_space=pl.ANY)
    grid_spec = pltpu.PrefetchScalarGridSpec(
        num_scalar_prefetch=2,
        grid=(ns,),
        in_specs=[pl.BlockSpec(memory_space=pltpu.SMEM),
                  pl.BlockSpec((1, 1, n_sel), lambda s, pt, sl: (s, 0, 0)),
                  row, hbm, hbm, hbm, hbm],
        out_specs=row,
        scratch_shapes=[pltpu.VMEM((2, n_sel, n_heads, HEAD_DIM), F32),
                        pltpu.VMEM((2, n_sel, n_heads, HEAD_DIM), F32),
                        pltpu.SemaphoreType.DMA((2, 2))])
    return pl.pallas_call(
        functools.partial(_sample_dsa_kernel, past=n_pages * PAGE_SIZE, n_heads=n_heads, head_off=head_off),
        grid_spec=grid_spec,
        out_shape=jax.ShapeDtypeStruct(qb.shape, F32),
        compiler_params=_cparams(("arbitrary",)),
    )(page_table, sel, rel_bias_t, sel.reshape(ns, 1, n_sel), qb, kb_new, vb_new, cache_kb, cache_vb)


def _split_weights(w_in, w_a, w_b, n_idx_heads, d):
    sizes = (w_a, w_a, w_a, w_b, w_b, w_b, n_idx_heads * IDX_DIM, IDX_DIM, n_idx_heads, d, d)
    offs = np.concatenate([[0], np.cumsum(sizes)])
    return [w_in[:, int(offs[i]):int(offs[i + 1])].astype(BF16) for i in range(len(sizes))]


def _layer_rows(x, mods, tm, tiles_per_group, lw, attend):
    (norm_mix_g, w_sec, qn_a, kn_a, qn_b, kn_b, w_pa, w_pb, w_out, norm_ffn_g, w_qp, keys1, keys2, pu, pv) = lw
    sh1, sc1, g1, sh2, sc2, g2 = mods
    r, d = x.shape
    xm = _prenorm(x, norm_mix_g, sc1, sh1, tm, tiles_per_group)
    w_qa, w_ka, w_va, w_qb, w_kb, w_vb, w_qi, w_ki, w_wi, w_ga, w_gb = w_sec
    qa = _linear(xm, w_qa, tm, qn_a)
    ka = _linear(xm, w_ka, tm, kn_a)
    va = _linear(xm, w_va, tm)
    qb = _linear(xm, w_qb, tm, qn_b)
    kb = _linear(xm, w_kb, tm, kn_b)
    vb = _linear(xm, w_vb, tm)
    qi = _linear(xm, w_qi, tm)
    ki = _linear(xm, w_ki, tm)
    wi_t = _linear_t(w_wi.T, xm, tm)
    ga = _linear(xm, w_ga, tm)
    gb = _linear(xm, w_gb, tm)
    out_a, out_b = attend(qa, ka, va, qb, kb, vb, qi, ki, wi_t)
    merged = _merge(out_a, out_b, ga, gb, w_pa, w_pb, tm)
    h, hm = _outproj(merged, w_out, x, g1, norm_ffn_g, sc2, sh2, tm, tiles_per_group)
    qp = _linear(hm, w_qp, tm)
    tm_peer = min(r, 512)
    e1n, e2, tau = _peer_route(qp, keys1, keys2, min(r, 256))
    peer = _peer_dense(hm, pu, pv, e1n, e2, tau, tm_peer, 512)
    y = _residual(h, peer, g2, tm, tiles_per_group)
    return y, (ka, va, kb, vb, ki)


def kernel(x_prompt, x_sample, cache_k_a, cache_v_a, cache_k_b, cache_v_b, cache_k_idx, page_table, c_prompt, c_sample, rel_bias, w_ada, b_ada, norm_mix_g, w_in, qnorm_a, knorm_a, qnorm_b, knorm_b, w_proj_a, w_proj_b, w_out, norm_ffn_g, w_q_peer, peer_keys1, peer_keys2, peer_u, peer_v):
    batch, seq, d = x_prompt.shape
    ns, dec_seq, _ = x_sample.shape
    depth = w_ada.shape[0]
    assert depth == 1 and dec_seq == 1 and seq % ATT_TILE == 0
    n_heads = d // 256
    n_idx_heads = d // 128
    w_att = n_heads * HEAD_DIM
    n_pages = page_table.shape[1]
    past = n_pages * PAGE_SIZE
    ns_pad = -(-ns // LANES) * LANES

    n_c = batch + ns_pad
    n_c_pad = -(-n_c // 8) * 8
    c_all = jnp.zeros((n_c_pad, d), F32).at[:batch].set(c_prompt).at[batch:batch + ns].set(c_sample)
    mod = _ada_mod(c_all, w_ada[0], b_ada[0])
    mods_p = tuple(mod[:batch, i * d:(i + 1) * d].reshape(batch, 1, d) for i in range(6))
    mods_s = tuple(mod[batch:batch + ns_pad, i * d:(i + 1) * d].reshape(1, ns_pad, d) for i in range(6))

    lw = (norm_mix_g[0], _split_weights(w_in[0], w_att, w_att, n_idx_heads, d),
          qnorm_a[0], knorm_a[0], qnorm_b[0], knorm_b[0],
          w_proj_a[0].astype(BF16), w_proj_b[0].astype(BF16), w_out[0].astype(BF16), norm_ffn_g[0],
          w_q_peer[0].astype(BF16), peer_keys1[0], peer_keys2[0],
          peer_u[0].astype(BF16), peer_v[0].astype(BF16))
    rel_bias_t = rel_bias.T
    tables = _bias_tables(rel_bias_t, seq)

    def attend_prompt(qa, ka, va, qb, kb, vb, qi, ki, wi_t):
        out_a = _moba_prompt(qa, ka, va, tables, batch, seq, n_heads)
        mask = _dsa_index(qi, wi_t, ki, batch, seq, n_idx_heads)
        out_b = _dsa_prompt(qb, kb, vb, tables, mask, batch, seq, n_heads, n_heads)
        return out_a, out_b

    tm_p = 512 if seq % 512 == 0 else ATT_TILE
    y_p, new_p = _layer_rows(x_prompt.reshape(batch * seq, d), mods_p, tm_p, seq // tm_p, lw, attend_prompt)

    def attend_sample(qa, ka, va, qb, kb, vb, qi, ki, wi_t):
        hd = lambda a: a[:ns].reshape(ns, n_heads, HEAD_DIM)
        wi = wi_t[:, :ns].T.reshape(ns, n_idx_heads, 1)
        blk, score = _sample_scan(page_table, cache_k_a, cache_k_idx, hd(qa),
                                  qi[:ns].reshape(ns, n_idx_heads, IDX_DIM), wi,
                                  ki[:ns].reshape(ns, 1, IDX_DIM), n_heads, n_idx_heads)
        blk_flat = blk[:, :MOBA_TOPK, :n_heads].transpose(0, 2, 1).reshape(ns, n_heads * MOBA_TOPK)
        out_a = _sample_moba(page_table, blk_flat, rel_bias_t, hd(qa), cache_k_a, cache_v_a, hd(ka), hd(va),
                             n_heads)
        n_sel = min(DSA_TOPK, (past + 1) // 4)
        sel = _sample_topk(score.reshape(ns, -1), n_sel, past + 1)
        out_b = _sample_dsa(page_table, sel, rel_bias_t, hd(qb), hd(kb), hd(vb), cache_k_b, cache_v_b,
                            n_heads, n_heads)
        pad = lambda a: jnp.zeros((ns_pad, w_att), F32).at[:ns].set(a.reshape(ns, w_att))
        return pad(out_a), pad(out_b)

    x_s = jnp.zeros((ns_pad, d), F32).at[:ns].set(x_sample.reshape(ns, d))
    y_s, new_s = _layer_rows(x_s, mods_s, ns_pad, 1, lw, attend_sample)

    def state(new, n_seq, t):
        ka, va, kb, vb, ki = new
        hd = lambda a: a[:n_seq * t].reshape(1, n_seq, t, n_heads, HEAD_DIM)
        return hd(ka), hd(va), hd(kb), hd(vb), ki[:n_seq * t].reshape(1, n_seq, t, IDX_DIM)

    return ((y_p.reshape(batch, seq, d), y_s[:ns].reshape(ns, 1, d))
            + state(new_p, batch, seq) + state(new_s, ns, 1))
```

```python
import functools
import math

import numpy as np
import jax
import jax.numpy as jnp
from jax import lax
from jax.experimental import pallas as pl
from jax.experimental.pallas import tpu as pltpu

F32 = jnp.float32
BF16 = jnp.bfloat16
I32 = jnp.int32
HIGHEST = lax.Precision.HIGHEST

HEAD_DIM = 128
PAGE_SIZE = 128
MOBA_BLOCK = 256
MOBA_TOPK = 3
DSA_TOPK = 256
IDX_DIM = 64
N_BUCKETS = 32
REL_MAX_DIST = 4096
PEER_NKEYS = 128
PEER_TOPK = 16
RMS_EPS = 1e-6
NEG = -1e30
LOG2E = math.log2(math.e)
LANES = 128
ATT_TILE = 256
VMEM_CAP = 56 * 1024 * 1024
HEADS_PER_STEP = 2
SCAN_BUFS = 8


def _bucket_thresholds():
    d = np.arange(0, 4 * REL_MAX_DIST + 2)
    max_exact = N_BUCKETS // 2
    nf = np.maximum(d, 1).astype(np.float32)
    large = max_exact + (np.log(nf / np.float32(max_exact)) / np.float32(math.log(REL_MAX_DIST / max_exact))
                         * np.float32(N_BUCKETS - max_exact)).astype(np.int32)
    large = np.minimum(large, N_BUCKETS - 1)
    bucket = np.where(d < max_exact, d, large)
    assert np.all(np.diff(bucket) >= 0) and bucket[-1] == N_BUCKETS - 1
    return [int(np.argmax(bucket >= b)) for b in range(1, N_BUCKETS)]


BUCKET_THR = _bucket_thresholds()


def _cparams(sem, vmem_bytes=None):
    kw = dict(dimension_semantics=sem)
    if vmem_bytes is not None:
        kw["vmem_limit_bytes"] = int(min(max(vmem_bytes, 16 * 1024 * 1024), VMEM_CAP))
    return pltpu.CompilerParams(**kw)


def _bias_from_dist(dist, bias_at):
    val = jnp.full(dist.shape, bias_at(0), F32)
    for b in range(1, N_BUCKETS):
        val = jnp.where(dist >= BUCKET_THR[b - 1], bias_at(b), val)
    return val


def _dot_nt(a, b, precision=None):
    return lax.dot_general(a, b, (((1,), (1,)), ((), ())), precision=precision, preferred_element_type=F32)


def _ada_kernel(c_ref, w_ref, b_ref, o_ref):
    c = c_ref[...]
    s = c * jax.nn.sigmoid(c)
    o_ref[...] = jnp.dot(s, w_ref[...], precision=HIGHEST, preferred_element_type=F32) + b_ref[...]


def _ada_mod(c, w_ada, b_ada):
    n, d = c.shape
    n6 = w_ada.shape[1]
    tn = 512
    return pl.pallas_call(
        _ada_kernel,
        grid=(n6 // tn,),
        in_specs=[pl.BlockSpec((n, d), lambda j: (0, 0)),
                  pl.BlockSpec((d, tn), lambda j: (0, j)),
                  pl.BlockSpec((1, tn), lambda j: (0, j))],
        out_specs=pl.BlockSpec((n, tn), lambda j: (0, j)),
        out_shape=jax.ShapeDtypeStruct((n, n6), F32),
        compiler_params=_cparams(("arbitrary",), 4 * d * tn * 4),
    )(c, w_ada, b_ada.reshape(1, n6))


def _prenorm_kernel(x_ref, g_ref, sc_ref, sh_ref, o_ref):
    x = x_ref[...]
    y = x * lax.rsqrt(jnp.mean(x * x, axis=-1, keepdims=True) + RMS_EPS) * g_ref[...]
    o_ref[...] = (y * (1.0 + sc_ref[0]) + sh_ref[0]).astype(o_ref.dtype)


def _mod_spec(mod, tm, tiles_per_group):
    g, r, d = mod.shape
    if r == 1:
        return pl.BlockSpec((1, 1, d), lambda i, *_: (i // tiles_per_group, 0, 0))
    return pl.BlockSpec((1, tm, d), lambda i, *_: (0, i, 0))


def _prenorm(x, gain, sc, sh, tm, tiles_per_group):
    r, d = x.shape
    return pl.pallas_call(
        _prenorm_kernel,
        grid=(r // tm,),
        in_specs=[pl.BlockSpec((tm, d), lambda i: (i, 0)),
                  pl.BlockSpec((1, d), lambda i: (0, 0)),
                  _mod_spec(sc, tm, tiles_per_group), _mod_spec(sh, tm, tiles_per_group)],
        out_specs=pl.BlockSpec((tm, d), lambda i: (i, 0)),
        out_shape=jax.ShapeDtypeStruct((r, d), BF16),
        compiler_params=_cparams(("parallel",), 8 * tm * d * 4),
    )(x, gain.reshape(1, d), sc, sh)


def _linear_kernel(x_ref, w_ref, *refs, headnorm):
    acc = jnp.dot(x_ref[...], w_ref[...], preferred_element_type=F32)
    if headnorm:
        g_ref, o_ref = refs
        for h in range(acc.shape[1] // HEAD_DIM):
            blk = acc[:, h * HEAD_DIM:(h + 1) * HEAD_DIM]
            ms = jnp.mean(blk * blk, axis=-1, keepdims=True)
            o_ref[:, h * HEAD_DIM:(h + 1) * HEAD_DIM] = blk * lax.rsqrt(ms + RMS_EPS) * g_ref[...]
    else:
        (o_ref,) = refs
        o_ref[...] = acc


def _linear(x, w, tm, head_gain=None):
    r, k = x.shape
    n = w.shape[1]
    in_specs = [pl.BlockSpec((tm, k), lambda i: (i, 0)), pl.BlockSpec((k, n), lambda i: (0, 0))]
    args = [x, w]
    if head_gain is not None:
        in_specs.append(pl.BlockSpec((1, HEAD_DIM), lambda i: (0, 0)))
        args.append(head_gain.reshape(1, HEAD_DIM))
    return pl.pallas_call(
        functools.partial(_linear_kernel, headnorm=head_gain is not None),
        grid=(r // tm,),
        in_specs=in_specs,
        out_specs=pl.BlockSpec((tm, n), lambda i: (i, 0)),
        out_shape=jax.ShapeDtypeStruct((r, n), F32),
        compiler_params=_cparams(("parallel",), 2 * (tm * k * 2 + k * n * 2 + tm * n * 4) + 4 * tm * n * 4),
    )(*args)


def _linear_t_kernel(w_ref, x_ref, o_ref):
    o_ref[...] = _dot_nt(w_ref[...], x_ref[...])


def _linear_t(w_t, x, tm):
    r, k = x.shape
    n = w_t.shape[0]
    return pl.pallas_call(
        _linear_t_kernel,
        grid=(r // tm,),
        in_specs=[pl.BlockSpec((n, k), lambda i: (0, 0)), pl.BlockSpec((tm, k), lambda i: (i, 0))],
        out_specs=pl.BlockSpec((n, tm), lambda i: (0, i)),
        out_shape=jax.ShapeDtypeStruct((n, r), F32),
        compiler_params=_cparams(("parallel",)),
    )(w_t, x)


def _bias_table_kernel(bias_ref, o_ref):
    h = pl.program_id(0)
    shape = o_ref.shape[1:]
    dist = lax.broadcasted_iota(I32, shape, 1) - lax.broadcasted_iota(I32, shape, 0)
    o_ref[0] = jnp.where(dist >= 0, _bias_from_dist(dist, lambda b: bias_ref[h, b] * LOG2E), NEG)


def _bias_tables(rel_bias_t, seq):
    nh = rel_bias_t.shape[0]
    return pl.pallas_call(
        _bias_table_kernel,
        grid=(nh,),
        in_specs=[pl.BlockSpec(memory_space=pltpu.SMEM)],
        out_specs=pl.BlockSpec((1, ATT_TILE, seq), lambda h: (h, 0, 0)),
        out_shape=jax.ShapeDtypeStruct((nh, ATT_TILE, seq), F32),
        compiler_params=_cparams(("parallel",), 4 * ATT_TILE * seq * 4),
    )(rel_bias_t)


def _flash_attend(n_trips, block_of, mask_of, q_ref, kbf_ref, vt_ref, tb_ref, o_ref, s_ref, p_ref, acc_ref):
    qt = pl.program_id(2)
    heads = _head_slices()
    qs = [(q_ref[:, hs] * (HEAD_DIM ** -0.5 * LOG2E)).astype(BF16) for hs in heads]

    def key_rows(blk):
        return pl.ds(pl.multiple_of(blk * ATT_TILE, ATT_TILE), ATT_TILE)

    first = block_of(0)
    for g, hs in enumerate(heads):
        s_ref[g] = _dot_nt(kbf_ref[key_rows(first), hs], qs[g])
        p_ref[g] = jnp.zeros(p_ref.shape[1:], p_ref.dtype)
        acc_ref[g] = jnp.zeros(acc_ref.shape[1:], acc_ref.dtype)

    def body(i, carries):
        blk = block_of(i)
        prev_rows = key_rows(block_of(jnp.maximum(i - 1, 0)))
        next_rows = key_rows(block_of(jnp.minimum(i + 1, n_trips - 1)))
        delta = pl.multiple_of((qt - blk) * ATT_TILE, ATT_TILE)
        out = []
        for g, hs in enumerate(heads):
            m, l = carries[g]
            s = s_ref[g] + tb_ref[g, :, pl.ds(delta, ATT_TILE)] + mask_of(blk, g)
            m_new = jnp.maximum(m, jnp.max(s, axis=0, keepdims=True))
            alpha = jnp.exp2(m - m_new)
            p = jnp.exp2(s - m_new)
            l = alpha * l + jnp.sum(p, axis=0, keepdims=True)
            pv = jnp.dot(vt_ref[hs, prev_rows], p_ref[g], preferred_element_type=F32)
            acc_ref[g] = (acc_ref[g] + pv) * alpha
            p_ref[g] = p.astype(BF16)
            s_ref[g] = _dot_nt(kbf_ref[next_rows, hs], qs[g])
            out.append((m_new, l))
        return tuple(out)

    m0 = jnp.full((1, ATT_TILE), NEG, F32)
    l0 = jnp.zeros((1, ATT_TILE), F32)
    carries = lax.fori_loop(0, n_trips, body, tuple((m0, l0) for _ in heads))
    last_rows = key_rows(block_of(n_trips - 1))
    for g, hs in enumerate(heads):
        _, l = carries[g]
        acc = acc_ref[g] + jnp.dot(vt_ref[hs, last_rows], p_ref[g], preferred_element_type=F32)
        o_ref[:, hs] = (acc / l).T


def _top_rows(s, k):
    n = s.shape[0]
    rows = lax.broadcasted_iota(I32, s.shape, 0)
    vals, picks = [], []
    for _ in range(k):
        m = jnp.max(s, axis=0, keepdims=True)
        first = jnp.min(jnp.where(s == m, rows, n), axis=0, keepdims=True)
        pick = rows == first
        s = jnp.where(pick, -jnp.inf, s)
        vals.append(m)
        picks.append(pick)
    return vals, picks


def _head_slices():
    return [slice(g * HEAD_DIM, (g + 1) * HEAD_DIM) for g in range(HEADS_PER_STEP)]


def _load_kv_scratch(k_ref, v_ref, kbf_ref, vt_ref):
    n_blk = k_ref.shape[0] // ATT_TILE
    for b in range(n_blk):
        sl = slice(b * ATT_TILE, (b + 1) * ATT_TILE)
        kbf_ref[sl, :] = k_ref[sl, :].astype(BF16)
        vt_ref[:, sl] = v_ref[sl, :].T.astype(BF16)


def _moba_kernel(q_ref, k_ref, v_ref, tb_ref, o_ref, kbf_ref, vt_ref, s_ref, p_ref, acc_ref, kmean_ref,
                 selb_ref):
    qt = pl.program_id(2)
    n_blk = k_ref.shape[0] // ATT_TILE
    heads = _head_slices()

    @pl.when(qt == 0)
    def _():
        _load_kv_scratch(k_ref, v_ref, kbf_ref, vt_ref)
        for b in range(n_blk):
            kmean_ref[b:b + 1, :] = jnp.mean(k_ref[b * ATT_TILE:(b + 1) * ATT_TILE, :], axis=0, keepdims=True)

    for g, hs in enumerate(heads):
        gate = _dot_nt(kmean_ref[:, hs], q_ref[:, hs], precision=HIGHEST)
        blocks = lax.broadcasted_iota(I32, gate.shape, 0)
        past = blocks < qt
        _, picks = _top_rows(jnp.where(past, gate, NEG), min(MOBA_TOPK, n_blk))
        sel = blocks == qt
        for p in picks:
            sel = sel | (p & past)
        selb_ref[g] = jnp.where(sel, 0.0, NEG)

    _flash_attend(qt + 1, lambda i: jnp.where(i == 0, qt, i - 1),
                  lambda blk, g: selb_ref[g, pl.ds(blk, 1), :],
                  q_ref, kbf_ref, vt_ref, tb_ref, o_ref, s_ref, p_ref, acc_ref)


def _attn_specs(seq, nq, table_off):
    wide = HEADS_PER_STEP * HEAD_DIM
    tile = pl.BlockSpec((ATT_TILE, wide), lambda b, h, t: (b * nq + t, h))
    full = pl.BlockSpec((seq, wide), lambda b, h, t: (b, h))
    table = pl.BlockSpec((HEADS_PER_STEP, ATT_TILE, seq), lambda b, h, t: (h + table_off, 0, 0))
    scratch = [pltpu.VMEM((seq, wide), BF16), pltpu.VMEM((wide, seq), BF16),
               pltpu.VMEM((HEADS_PER_STEP, ATT_TILE, ATT_TILE), F32),
               pltpu.VMEM((HEADS_PER_STEP, ATT_TILE, ATT_TILE), BF16),
               pltpu.VMEM((HEADS_PER_STEP, HEAD_DIM, ATT_TILE), F32)]
    vmem = 2 * (2 * seq * wide * 4 + HEADS_PER_STEP * ATT_TILE * seq * 4) + 2 * seq * wide * 2
    return tile, full, table, scratch, vmem


def _moba_prompt(q, k, v, tables, batch, seq, n_heads):
    nq = seq // ATT_TILE
    n_blk = seq // MOBA_BLOCK
    tile, full, table, scratch, vmem = _attn_specs(seq, nq, 0)
    return pl.pallas_call(
        _moba_kernel,
        grid=(batch, n_heads // HEADS_PER_STEP, nq),
        in_specs=[tile, full, full, table],
        out_specs=tile,
        out_shape=jax.ShapeDtypeStruct(q.shape, F32),
        scratch_shapes=scratch + [pltpu.VMEM((n_blk, HEADS_PER_STEP * HEAD_DIM), F32),
                                  pltpu.VMEM((HEADS_PER_STEP, n_blk, ATT_TILE), F32)],
        compiler_params=_cparams(("parallel", "parallel", "arbitrary"), vmem + 8 * 1024 * 1024),
    )(q, k, v, tables)


def _sortable(x):
    bits = lax.bitcast_convert_type(x, I32)
    return bits ^ ((bits >> 31) & 0x7FFFFFFF)


def _dsa_index_kernel(qi_ref, wt_ref, ki_ref, o_ref, u_ref, *, n_sel, n_idx_heads):
    qt = pl.program_id(1)
    nq = ki_ref.shape[0] // ATT_TILE
    w = wt_ref[...] * (IDX_DIM ** -0.5 * n_idx_heads ** -0.5)
    qi = qi_ref[...].astype(BF16)
    rows = lax.broadcasted_iota(I32, (ATT_TILE, ATT_TILE), 0)
    cols = lax.broadcasted_iota(I32, (ATT_TILE, ATT_TILE), 1)

    def score_body(kb, _):
        kk = pl.multiple_of(kb * ATT_TILE, ATT_TILE)
        kblk = ki_ref[pl.ds(kk, ATT_TILE), :].astype(BF16)
        sc = jnp.zeros((ATT_TILE, ATT_TILE), F32)
        for h in range(n_idx_heads):
            sh = _dot_nt(kblk, qi[:, h * IDX_DIM:(h + 1) * IDX_DIM])
            sc = sc + jnp.maximum(sh, 0.0) * w[h:h + 1, :]
        sc = jnp.where((kb < qt) | (rows <= cols), sc, NEG)
        u_ref[pl.ds(kk, ATT_TILE), :] = _sortable(sc)
        return 0

    lax.fori_loop(0, qt + 1, score_body, 0)

    def bit_body(i, t):
        cand = t + lax.shift_left(jnp.int32(1), 31 - i)

        def count_body(kb, cnt):
            kk = pl.multiple_of(kb * ATT_TILE, ATT_TILE)
            hit = (u_ref[pl.ds(kk, ATT_TILE), :] >= cand).astype(I32)
            return cnt + jnp.sum(hit, axis=0, keepdims=True)

        cnt = lax.fori_loop(0, qt + 1, count_body, jnp.zeros((1, ATT_TILE), I32))
        return jnp.where(cnt >= n_sel, cand, t)

    thr = lax.fori_loop(0, 32, bit_body, jnp.full((1, ATT_TILE), -2 ** 31, I32))

    def out_body(kb, _):
        kk = pl.multiple_of(kb * ATT_TILE, ATT_TILE)
        keep = (u_ref[pl.ds(kk, ATT_TILE), :] >= thr) & ((kb < qt) | (rows <= cols))
        o_ref[0, pl.ds(kk, ATT_TILE), :] = jnp.where(keep, 0.0, NEG).astype(o_ref.dtype)
        return 0

    def future_body(kb, _):
        kk = pl.multiple_of(kb * ATT_TILE, ATT_TILE)
        o_ref[0, pl.ds(kk, ATT_TILE), :] = jnp.full((ATT_TILE, ATT_TILE), NEG, o_ref.dtype)
        return 0

    lax.fori_loop(0, qt + 1, out_body, 0)
    lax.fori_loop(qt + 1, nq, future_body, 0)


def _dsa_index(qi, wi_t, ki, batch, seq, n_idx_heads):
    nq = seq // ATT_TILE
    n_sel = min(DSA_TOPK, seq // 4)
    return pl.pallas_call(
        functools.partial(_dsa_index_kernel, n_sel=n_sel, n_idx_heads=n_idx_heads),
        grid=(batch, nq),
        in_specs=[pl.BlockSpec((ATT_TILE, qi.shape[1]), lambda b, t: (b * nq + t, 0)),
                  pl.BlockSpec((n_idx_heads, ATT_TILE), lambda b, t: (0, b * nq + t)),
                  pl.BlockSpec((seq, IDX_DIM), lambda b, t: (b, 0))],
        out_specs=pl.BlockSpec((1, seq, ATT_TILE), lambda b, t: (b, 0, t)),
        out_shape=jax.ShapeDtypeStruct((batch, seq, seq), BF16),
        scratch_shapes=[pltpu.VMEM((seq, ATT_TILE), I32)],
        compiler_params=_cparams(("parallel", "arbitrary"), 32 * 1024 * 1024),
    )(qi, wi_t, ki)


def _dsa_attn_kernel(q_ref, k_ref, v_ref, tb_ref, mask_ref, o_ref, kbf_ref, vt_ref, s_ref, p_ref, acc_ref):
    qt = pl.program_id(2)

    @pl.when(qt == 0)
    def _():
        _load_kv_scratch(k_ref, v_ref, kbf_ref, vt_ref)

    def mask_of(blk, g):
        return mask_ref[0, pl.ds(pl.multiple_of(blk * ATT_TILE, ATT_TILE), ATT_TILE), :].astype(F32)

    _flash_attend(qt + 1, lambda i: i, mask_of, q_ref, kbf_ref, vt_ref, tb_ref, o_ref, s_ref, p_ref, acc_ref)


def _dsa_prompt(q, k, v, tables, mask, batch, seq, n_heads, head_off):
    nq = seq // ATT_TILE
    tile, full, table, scratch, vmem = _attn_specs(seq, nq, head_off // HEADS_PER_STEP)
    return pl.pallas_call(
        _dsa_attn_kernel,
        grid=(batch, n_heads // HEADS_PER_STEP, nq),
        in_specs=[tile, full, full, table, pl.BlockSpec((1, seq, ATT_TILE), lambda b, h, t: (b, 0, t))],
        out_specs=tile,
        out_shape=jax.ShapeDtypeStruct(q.shape, F32),
        scratch_shapes=scratch,
        compiler_params=_cparams(("parallel", "parallel", "arbitrary"),
                                 vmem + 2 * seq * ATT_TILE * 2 + 8 * 1024 * 1024),
    )(q, k, v, tables, mask)


def _merge_kernel(oa_ref, ob_ref, ga_ref, gb_ref, wa_ref, wb_ref, o_ref):
    bra = jnp.dot(oa_ref[...].astype(BF16), wa_ref[...], preferred_element_type=F32)
    brb = jnp.dot(ob_ref[...].astype(BF16), wb_ref[...], preferred_element_type=F32)
    o_ref[...] = (jax.nn.sigmoid(ga_ref[...]) * bra + jax.nn.sigmoid(gb_ref[...]) * brb).astype(o_ref.dtype)


def _merge(out_a, out_b, ga, gb, w_pa, w_pb, tm):
    r, w = out_a.shape
    d = ga.shape[1]
    row = lambda n: pl.BlockSpec((tm, n), lambda i: (i, 0))
    full = lambda a: pl.BlockSpec(a.shape, lambda i: (0, 0))
    return pl.pallas_call(
        _merge_kernel,
        grid=(r // tm,),
        in_specs=[row(w), row(w), row(d), row(d), full(w_pa), full(w_pb)],
        out_specs=row(d),
        out_shape=jax.ShapeDtypeStruct((r, d), BF16),
        compiler_params=_cparams(("parallel",), 4 * w * d * 2 + 8 * tm * d * 4),
    )(out_a, out_b, ga, gb, w_pa, w_pb)


def _outproj_kernel(mg_ref, w_ref, x_ref, g1_ref, ng_ref, sc_ref, sh_ref, h_ref, hm_ref):
    h = x_ref[...] + g1_ref[0] * jnp.dot(mg_ref[...], w_ref[...], preferred_element_type=F32)
    h_ref[...] = h
    y = h * lax.rsqrt(jnp.mean(h * h, axis=-1, keepdims=True) + RMS_EPS) * ng_ref[...]
    hm_ref[...] = (y * (1.0 + sc_ref[0]) + sh_ref[0]).astype(hm_ref.dtype)


def _outproj(merged, w_out, x, g1, norm_g, sc2, sh2, tm, tiles_per_group):
    r, d = x.shape
    row = pl.BlockSpec((tm, d), lambda i: (i, 0))
    ms = lambda m: _mod_spec(m, tm, tiles_per_group)
    return pl.pallas_call(
        _outproj_kernel,
        grid=(r // tm,),
        in_specs=[row, pl.BlockSpec((d, d), lambda i: (0, 0)), row, ms(g1),
                  pl.BlockSpec((1, d), lambda i: (0, 0)), ms(sc2), ms(sh2)],
        out_specs=[row, row],
        out_shape=[jax.ShapeDtypeStruct((r, d), F32), jax.ShapeDtypeStruct((r, d), BF16)],
        compiler_params=_cparams(("parallel",), 4 * d * d * 2 + 12 * tm * d * 4),
    )(merged, w_out, x, g1, norm_g.reshape(1, d), sc2, sh2)


def _peer_pairs(k):
    return [(i, j) for i in range(k) for j in range(k) if (i + 1) * (j + 1) <= k]


def _top_rows_all_copies(s, k):
    rest, vals = s, []
    for _ in range(k):
        m = jnp.max(rest, axis=0, keepdims=True)
        rest = jnp.where(rest == m, -jnp.inf, rest)
        vals.append(m)
    dropped = jnp.sum((rest == -jnp.inf).astype(I32), axis=0, keepdims=True)
    return vals, jnp.abs(dropped - k)


def _peer_route_kernel(q_ref, k1_ref, k2_ref, e1_ref, e2_ref, tau_ref):
    n_heads = k1_ref.shape[0]
    dk = k1_ref.shape[2]
    kk = PEER_TOPK + 1

    def route(top_values):
        off = jnp.zeros((1, q_ref.shape[0]), I32)
        for h in range(n_heads):
            q1 = q_ref[:, (2 * h) * dk:(2 * h + 1) * dk]
            q2 = q_ref[:, (2 * h + 1) * dk:(2 * h + 2) * dk]
            s1 = _dot_nt(k1_ref[h], q1, precision=HIGHEST)
            s2 = _dot_nt(k2_ref[h], q2, precision=HIGHEST)
            v1, d1 = top_values(s1, kk)
            v2, d2 = top_values(s2, kk)
            cand = jnp.concatenate([v1[i] + v2[j] for i, j in _peer_pairs(kk)], axis=0)
            c, d3 = top_values(cand, kk)
            off = off + d1 + d2 + d3
            top = c[0]
            z = sum(jnp.exp(ck - top) for ck in c[:PEER_TOPK])
            inv_z = 1.0 / z
            e1_ref[h] = jnp.exp(s1 - v1[0]) * inv_z
            e2_ref[h] = jnp.exp(s2 - v2[0])
            tau_ref[h:h + 1, :] = jnp.exp(0.5 * (c[PEER_TOPK - 1] + c[PEER_TOPK]) - top) * inv_z
        return off

    off = route(_top_rows_all_copies)

    @pl.when(jnp.max(off) > 0)
    def _():
        route(lambda s, k: (_top_rows(s, k)[0], 0))


def _peer_route(qp, keys1, keys2, tm):
    r, d = qp.shape
    nh, nk, dk = keys1.shape
    return pl.pallas_call(
        _peer_route_kernel,
        grid=(r // tm,),
        in_specs=[pl.BlockSpec((tm, d), lambda i: (i, 0)),
                  pl.BlockSpec((nh, nk, dk), lambda i: (0, 0, 0)),
                  pl.BlockSpec((nh, nk, dk), lambda i: (0, 0, 0))],
        out_specs=[pl.BlockSpec((nh, nk, tm), lambda i: (0, 0, i)),
                   pl.BlockSpec((nh, nk, tm), lambda i: (0, 0, i)),
                   pl.BlockSpec((nh, tm), lambda i: (0, i))],
        out_shape=[jax.ShapeDtypeStruct((nh, nk, r), F32), jax.ShapeDtypeStruct((nh, nk, r), F32),
                   jax.ShapeDtypeStruct((nh, r), F32)],
        compiler_params=_cparams(("parallel",), 32 * 1024 * 1024),
    )(qp, keys1, keys2)


def _gelu_tanh(x):
    return 0.5 * x * (1.0 + jnp.tanh(math.sqrt(2.0 / math.pi) * (x + 0.044715 * (x * x * x))))


def _peer_dense_kernel(hm_ref, u_ref, v_ref, e1_ref, e2_ref, tau_ref, o_ref, act_a, act_b):
    j = pl.program_id(1)
    n_tiles = pl.num_programs(1) - 1
    te = u_ref.shape[0]
    n_heads = e1_ref.shape[0]
    rows_per_tile = te // PEER_NKEYS

    d = hm_ref.shape[1]
    rows_per_dot = min(rows_per_tile, 2)
    k_chunk = d // rows_per_tile

    def step(dst, src):
        proj, total, pieces = None, None, []
        for r in range(rows_per_tile):
            if dst is not None:
                ks = slice(r * k_chunk, (r + 1) * k_chunk)
                part = _dot_nt(u_ref[:, ks], hm_ref[:, ks])
                proj = part if proj is None else proj + part
            if src is not None:
                i1 = (j - 1) * rows_per_tile + r
                g = jnp.zeros((PEER_NKEYS, src.shape[1]), F32)
                for h in range(n_heads):
                    w = e2_ref[h] * e1_ref[h, pl.ds(i1, 1), :]
                    g = g + jnp.where(w >= tau_ref[h:h + 1, :], w, 0.0)
                a_t = g * _gelu_tanh(src[r * PEER_NKEYS:(r + 1) * PEER_NKEYS, :])
                pieces.append(a_t.T.astype(BF16))
                if len(pieces) == rows_per_dot:
                    a = jnp.concatenate(pieces, axis=1) if rows_per_dot > 1 else pieces[0]
                    r0 = r + 1 - rows_per_dot
                    part = jnp.dot(a, v_ref[r0 * PEER_NKEYS:(r + 1) * PEER_NKEYS, :], preferred_element_type=F32)
                    total = part if total is None else total + part
                    pieces = []
        if dst is not None:
            dst[...] = proj
        if src is not None:
            o_ref[...] += total

    @pl.when(j == 0)
    def _():
        o_ref[...] = jnp.zeros_like(o_ref)
        step(act_a, None)

    @pl.when((j > 0) & (j < n_tiles) & (j % 2 == 0))
    def _():
        step(act_a, act_b)

    @pl.when((j > 0) & (j < n_tiles) & (j % 2 == 1))
    def _():
        step(act_b, act_a)

    @pl.when((j == n_tiles) & (j % 2 == 0))
    def _():
        step(None, act_b)

    @pl.when((j == n_tiles) & (j % 2 == 1))
    def _():
        step(None, act_a)


def _peer_dense(hm, u, v, e1n, e2, tau, tm, te):
    r, d = hm.shape
    n_tiles = u.shape[0] // te
    nh, nk, _ = e1n.shape
    return pl.pallas_call(
        _peer_dense_kernel,
        grid=(r // tm, n_tiles + 1),
        in_specs=[pl.BlockSpec((tm, d), lambda i, j: (i, 0)),
                  pl.BlockSpec((te, d), lambda i, j: (jnp.minimum(j, n_tiles - 1), 0)),
                  pl.BlockSpec((te, d), lambda i, j: (jnp.maximum(j - 1, 0), 0)),
                  pl.BlockSpec((nh, nk, tm), lambda i, j: (0, 0, i)),
                  pl.BlockSpec((nh, nk, tm), lambda i, j: (0, 0, i)),
                  pl.BlockSpec((nh, tm), lambda i, j: (0, i))],
        out_specs=pl.BlockSpec((tm, d), lambda i, j: (i, 0)),
        out_shape=jax.ShapeDtypeStruct((r, d), F32),
        scratch_shapes=[pltpu.VMEM((te, tm), F32), pltpu.VMEM((te, tm), F32)],
        compiler_params=_cparams(("parallel", "arbitrary"),
                                 2 * (tm * d * 2 + 2 * te * d * 2 + 2 * nh * nk * tm * 4 + tm * d * 4)
                                 + 10 * te * tm * 4),
    )(hm, u, v, e1n, e2, tau)


def _residual_kernel(h_ref, p_ref, g2_ref, o_ref):
    o_ref[...] = h_ref[...] + g2_ref[0] * p_ref[...]


def _residual(h, p, g2, tm, tiles_per_group):
    r, d = h.shape
    row = pl.BlockSpec((tm, d), lambda i: (i, 0))
    return pl.pallas_call(
        _residual_kernel,
        grid=(r // tm,),
        in_specs=[row, row, _mod_spec(g2, tm, tiles_per_group)],
        out_specs=row,
        out_shape=jax.ShapeDtypeStruct((r, d), F32),
        compiler_params=_cparams(("parallel",), 8 * tm * d * 4),
    )(h, p, g2)


def _sample_scan_kernel(pt_ref, qa_ref, qi_ref, wi_ref, kinew_ref, ka_hbm, kidx_hbm, blk_ref, score_ref,
                        kbuf, ibuf, ksum_ref, ksem, isem, *, n_idx_heads, n_heads):
    ns, n_pages = pt_ref.shape
    total = ns * n_pages
    pages_per_blk = MOBA_BLOCK // PAGE_SIZE
    n_blk = ksum_ref.shape[0]
    n_bufs = kbuf.shape[0]

    def page_copies(g, slot):
        s = g // n_pages
        page = pt_ref[s, g - s * n_pages]
        return (pltpu.make_async_copy(ka_hbm.at[0, page], kbuf.at[slot], ksem.at[slot]),
                pltpu.make_async_copy(kidx_hbm.at[0, page], ibuf.at[slot], isem.at[slot]))

    for g in range(n_bufs):
        for c in page_copies(g, g):
            c.start()

    def body(g, _):
        slot = g % n_bufs
        s = g // n_pages
        p = g - s * n_pages
        for c in page_copies(g, slot):
            c.wait()

        @pl.when(p == 0)
        def _():
            ksum_ref[...] = jnp.zeros_like(ksum_ref)
            score_ref[s, pl.ds(n_pages, 8), :] = jnp.full((8, PAGE_SIZE), -jnp.inf, F32)

        ksum_ref[p // pages_per_blk] += jnp.sum(kbuf[slot], axis=0)

        w = wi_ref[s] * (IDX_DIM ** -0.5 * n_idx_heads ** -0.5)
        qi = qi_ref[s]
        sc = jnp.maximum(_dot_nt(qi.astype(BF16), ibuf[slot].astype(BF16)), 0.0) * w
        score_ref[s, pl.ds(p, 1), :] = jnp.sum(sc, axis=0, keepdims=True)

        @pl.when(g + n_bufs < total)
        def _():
            for c in page_copies(g + n_bufs, slot):
                c.start()

        @pl.when(p == n_pages - 1)
        def _():
            kn = kinew_ref[s].astype(BF16).astype(F32)
            dots = jnp.sum(qi.astype(BF16).astype(F32) * kn, axis=-1, keepdims=True)
            new_score = jnp.sum(jnp.maximum(dots, 0.0) * w, axis=0, keepdims=True)
            lane = lax.broadcasted_iota(I32, (1, PAGE_SIZE), 1)
            score_ref[s, pl.ds(n_pages, 1), :] = jnp.where(lane == 0, new_score, -jnp.inf)

            gate = jnp.sum(ksum_ref[...] * (qa_ref[s] * (1.0 / MOBA_BLOCK)), axis=-1)
            rows = lax.broadcasted_iota(I32, gate.shape, 0)
            out = jnp.zeros(blk_ref.shape[1:], I32)
            orow = lax.broadcasted_iota(I32, out.shape, 0)
            for t in range(MOBA_TOPK):
                m = jnp.max(gate, axis=0, keepdims=True)
                first = jnp.min(jnp.where(gate == m, rows, n_blk), axis=0, keepdims=True)
                gate = jnp.where(rows == first, -jnp.inf, gate)
                padded = jnp.concatenate([first, jnp.zeros((1, out.shape[1] - n_heads), I32)], axis=1)
                out = jnp.where(orow == t, padded, out)
            blk_ref[s] = out
        return 0

    lax.fori_loop(0, total, body, 0)


def _sample_scan(page_table, cache_ka, cache_kidx, qa, qi, wi, ki_new, n_heads, n_idx_heads):
    ns, n_pages = page_table.shape
    n_blk = n_pages * PAGE_SIZE // MOBA_BLOCK
    n_bufs = min(SCAN_BUFS, ns * n_pages)
    whole = lambda a: pl.BlockSpec(a.shape, lambda i, pt: (0,) * a.ndim)
    hbm = pl.BlockSpec(memory_space=pl.ANY)
    out_shape = [jax.ShapeDtypeStruct((ns, 8, LANES), I32),
                 jax.ShapeDtypeStruct((ns, n_pages + 8, PAGE_SIZE), F32)]
    grid_spec = pltpu.PrefetchScalarGridSpec(
        num_scalar_prefetch=1,
        grid=(1,),
        in_specs=[whole(qa), whole(qi), whole(wi), whole(ki_new), hbm, hbm],
        out_specs=[whole(o) for o in out_shape],
        scratch_shapes=[pltpu.VMEM((n_bufs, PAGE_SIZE, n_heads, HEAD_DIM), F32),
                        pltpu.VMEM((n_bufs, PAGE_SIZE, IDX_DIM), F32),
                        pltpu.VMEM((n_blk, n_heads, HEAD_DIM), F32),
                        pltpu.SemaphoreType.DMA((n_bufs,)), pltpu.SemaphoreType.DMA((n_bufs,))])
    return pl.pallas_call(
        functools.partial(_sample_scan_kernel, n_idx_heads=n_idx_heads, n_heads=n_heads),
        grid_spec=grid_spec,
        out_shape=out_shape,
        compiler_params=_cparams(("arbitrary",), 24 * 1024 * 1024),
    )(page_table, qa, qi, wi, ki_new, cache_ka, cache_kidx)


def _sample_topk_kernel(score_ref, idx_ref, *, n_sel, n_valid):
    s = score_ref[...]
    n = s.shape[1]
    lane = lax.broadcasted_iota(I32, s.shape, 1)
    s = jnp.where(lane < n_valid, s, -jnp.inf)
    out_lane = lax.broadcasted_iota(I32, idx_ref.shape, 1)

    def body(t, carry):
        s, out = carry
        m = jnp.max(s, axis=1, keepdims=True)
        first = jnp.min(jnp.where(s == m, lane, n), axis=1, keepdims=True)
        s = jnp.where(lane == first, -jnp.inf, s)
        return s, jnp.where(out_lane == t, first, out)

    _, out = lax.fori_loop(0, n_sel, body, (s, jnp.zeros(idx_ref.shape, I32)))
    idx_ref[...] = out


def _sample_topk(score, n_sel, n_valid):
    ns, n = score.shape
    return pl.pallas_call(
        functools.partial(_sample_topk_kernel, n_sel=n_sel, n_valid=n_valid),
        out_shape=jax.ShapeDtypeStruct((ns, n_sel), I32),
        compiler_params=_cparams(None, 8 * ns * n * 4),
    )(score)


def _sample_moba_kernel(pt_ref, blk_ref, bias_ref, q_ref, kn_ref, vn_ref, ck_hbm, cv_hbm, o_ref,
                        kbuf, vbuf, sem, *, past, n_heads):
    s_id = pl.program_id(0)
    pages_per_blk = MOBA_BLOCK // PAGE_SIZE
    slabs = [(h, t, half) for h in range(n_heads) for t in range(MOBA_TOPK) for half in range(pages_per_blk)]

    def slab_copies(i):
        h, t, half = slabs[i]
        page = pt_ref[s_id, blk_ref[s_id, h * MOBA_TOPK + t] * pages_per_blk + half]
        return (pltpu.make_async_copy(ck_hbm.at[0, page, pl.ds(0, PAGE_SIZE), h], kbuf.at[i], sem.at[0]),
                pltpu.make_async_copy(cv_hbm.at[0, page, pl.ds(0, PAGE_SIZE), h], vbuf.at[i], sem.at[1]))

    for i in range(len(slabs)):
        for c in slab_copies(i):
            c.start()
    for i in range(len(slabs)):
        for c in slab_copies(i):
            c.wait()

    lane = lax.broadcasted_iota(I32, (1, PAGE_SIZE), 1)
    outs = []
    for h in range(n_heads):
        q8 = jnp.broadcast_to(q_ref[0, h:h + 1, :] * HEAD_DIM ** -0.5, (8, HEAD_DIM)).astype(BF16)
        kn = kn_ref[0, h:h + 1, :].astype(BF16).astype(F32)
        s_own = jnp.sum(q8[0:1].astype(F32) * kn, axis=-1, keepdims=True) + bias_ref[h, 0]
        logits = []
        m = s_own
        for i, (hh, t, half) in enumerate(slabs):
            if hh != h:
                continue
            key0 = blk_ref[s_id, h * MOBA_TOPK + t] * MOBA_BLOCK + half * PAGE_SIZE
            dist = past - (key0 + lane)
            s = _dot_nt(q8, kbuf[i].astype(BF16))[0:1] + _bias_from_dist(dist, lambda b: bias_ref[h, b])
            s = jnp.where(dist >= 0, s, NEG)
            logits.append((i, s))
            m = jnp.maximum(m, jnp.max(s, axis=-1, keepdims=True))
        p_own = jnp.exp(s_own - m)
        l = p_own
        acc = p_own * vn_ref[0, h:h + 1, :].astype(BF16).astype(F32)
        for i, s in logits:
            p = jnp.exp(s - m)
            l = l + jnp.sum(p, axis=-1, keepdims=True)
            p8 = jnp.broadcast_to(p, (8, PAGE_SIZE)).astype(BF16)
            acc = acc + jnp.dot(p8, vbuf[i].astype(BF16), preferred_element_type=F32)[0:1]
        outs.append(acc / l)
    o_ref[0] = jnp.concatenate(outs, axis=0)


def _sample_moba(page_table, blk_flat, rel_bias_t, qa, cache_ka, cache_va, ka_new, va_new, n_heads):
    ns, n_pages = page_table.shape
    n_slabs = n_heads * MOBA_TOPK * (MOBA_BLOCK // PAGE_SIZE)
    heads = pl.BlockSpec((1, n_heads, HEAD_DIM), lambda s, pt, blk: (s, 0, 0))
    hbm = pl.BlockSpec(memory_space=pl.ANY)
    grid_spec = pltpu.PrefetchScalarGridSpec(
        num_scalar_prefetch=2,
        grid=(ns,),
        in_specs=[pl.BlockSpec(memory_space=pltpu.SMEM), heads, heads, heads, hbm, hbm],
        out_specs=heads,
        scratch_shapes=[pltpu.VMEM((n_slabs, PAGE_SIZE, HEAD_DIM), F32),
                        pltpu.VMEM((n_slabs, PAGE_SIZE, HEAD_DIM), F32),
                        pltpu.SemaphoreType.DMA((2,))])
    return pl.pallas_call(
        functools.partial(_sample_moba_kernel, past=n_pages * PAGE_SIZE, n_heads=n_heads),
        grid_spec=grid_spec,
        out_shape=jax.ShapeDtypeStruct(qa.shape, F32),
        compiler_params=_cparams(("arbitrary",), 24 * 1024 * 1024),
    )(page_table, blk_flat, rel_bias_t, qa, ka_new, va_new, cache_ka, cache_va)


def _sample_dsa_kernel(pt_ref, sel_ref, bias_ref, selv_ref, q_ref, kn_ref, vn_ref, ck_ref, cv_ref, o_ref,
                       kbuf, vbuf, sem, *, past, n_heads, head_off):
    s_id = pl.program_id(0)
    n_samples = pl.num_programs(0)
    n_sel = kbuf.shape[1]
    slot = s_id % 2

    def row_copies(sample, buf, r):
        idx = sel_ref[sample, r]
        page = pt_ref[sample, jnp.minimum(idx, past - 1) // PAGE_SIZE]
        pos = idx % PAGE_SIZE
        dst_k, dst_v = kbuf.at[buf, pl.ds(r, 1)], vbuf.at[buf, pl.ds(r, 1)]
        cached = (pltpu.make_async_copy(ck_ref.at[0, page, pl.ds(pos, 1)], dst_k, sem.at[buf, 0]),
                  pltpu.make_async_copy(cv_ref.at[0, page, pl.ds(pos, 1)], dst_v, sem.at[buf, 1]))
        fresh = (pltpu.make_async_copy(kn_ref.at[pl.ds(sample, 1)], dst_k, sem.at[buf, 0]),
                 pltpu.make_async_copy(vn_ref.at[pl.ds(sample, 1)], dst_v, sem.at[buf, 1]))
        return idx, cached, fresh

    def gather(sample, buf, action):
        def body(r, _):
            idx, cached, fresh = row_copies(sample, buf, r)

            @pl.when(idx < past)
            def _():
                for c in cached:
                    action(c)

            @pl.when(idx >= past)
            def _():
                for c in fresh:
                    action(c)
            return 0

        lax.fori_loop(0, n_sel, body, 0)

    @pl.when(s_id == 0)
    def _():
        gather(s_id, slot, lambda c: c.start())

    @pl.when(s_id + 1 < n_samples)
    def _():
        gather(s_id + 1, 1 - slot, lambda c: c.start())

    gather(s_id, slot, lambda c: c.wait())

    dist = past - selv_ref[0]
    outs = []
    for h in range(n_heads):
        q8 = jnp.broadcast_to(q_ref[0, h:h + 1, :] * HEAD_DIM ** -0.5, (8, HEAD_DIM)).astype(BF16)
        s = _dot_nt(q8, kbuf[slot, :, h, :].astype(BF16))[0:1]
        s = s + _bias_from_dist(dist, lambda b: bias_ref[h + head_off, b])
        s = jnp.where(dist >= 0, s, NEG)
        p = jnp.exp(s - jnp.max(s, axis=-1, keepdims=True))
        l = jnp.sum(p, axis=-1, keepdims=True)
        p8 = jnp.broadcast_to(p, (8, n_sel)).astype(BF16)
        outs.append(jnp.dot(p8, vbuf[slot, :, h, :].astype(BF16), preferred_element_type=F32)[0:1] / l)
    o_ref[0] = jnp.concatenate(outs, axis=0)


def _sample_dsa(page_table, sel, rel_bias_t, qb, kb_new, vb_new, cache_kb, cache_vb, n_heads, head_off):
    ns, n_pages = page_table.shape
    n_sel = sel.shape[1]
    row = pl.BlockSpec((1, n_heads, HEAD_DIM), lambda s, pt, sl: (s, 0, 0))
    hbm = pl.BlockSpec(memory_space=pl.ANY)
    grid_spec = pltpu.PrefetchScalarGridSpec(
        num_scalar_prefetch=2,
        grid=(ns,),
        in_specs=[pl.BlockSpec(memory_space=pltpu.SMEM),
                  pl.BlockSpec((1, 1, n_sel), lambda s, pt, sl: (s, 0, 0)),
                  row, hbm, hbm, hbm, hbm],
        out_specs=row,
        scratch_shapes=[pltpu.VMEM((2, n_sel, n_heads, HEAD_DIM), F32),
                        pltpu.VMEM((2, n_sel, n_heads, HEAD_DIM), F32),
                        pltpu.SemaphoreType.DMA((2, 2))])
    return pl.pallas_call(
        functools.partial(_sample_dsa_kernel, past=n_pages * PAGE_SIZE, n_heads=n_heads, head_off=head_off),
        grid_spec=grid_spec,
        out_shape=jax.ShapeDtypeStruct(qb.shape, F32),
        compiler_params=_cparams(("arbitrary",)),
    )(page_table, sel, rel_bias_t, sel.reshape(ns, 1, n_sel), qb, kb_new, vb_new, cache_kb, cache_vb)


def _split_weights(w_in, w_a, w_b, n_idx_heads, d):
    sizes = (w_a, w_a, w_a, w_b, w_b, w_b, n_idx_heads * IDX_DIM, IDX_DIM, n_idx_heads, d, d)
    offs = np.concatenate([[0], np.cumsum(sizes)])
    return [w_in[:, int(offs[i]):int(offs[i + 1])].astype(BF16) for i in range(len(sizes))]


def _layer_rows(x, mods, tm, tiles_per_group, lw, attend):
    (norm_mix_g, w_sec, qn_a, kn_a, qn_b, kn_b, w_pa, w_pb, w_out, norm_ffn_g, w_qp, keys1, keys2, pu, pv) = lw
    sh1, sc1, g1, sh2, sc2, g2 = mods
    r, d = x.shape
    xm = _prenorm(x, norm_mix_g, sc1, sh1, tm, tiles_per_group)
    w_qa, w_ka, w_va, w_qb, w_kb, w_vb, w_qi, w_ki, w_wi, w_ga, w_gb = w_sec
    qa = _linear(xm, w_qa, tm, qn_a)
    ka = _linear(xm, w_ka, tm, kn_a)
    va = _linear(xm, w_va, tm)
    qb = _linear(xm, w_qb, tm, qn_b)
    kb = _linear(xm, w_kb, tm, kn_b)
    vb = _linear(xm, w_vb, tm)
    qi = _linear(xm, w_qi, tm)
    ki = _linear(xm, w_ki, tm)
    wi_t = _linear_t(w_wi.T, xm, tm)
    ga = _linear(xm, w_ga, tm)
    gb = _linear(xm, w_gb, tm)
    out_a, out_b = attend(qa, ka, va, qb, kb, vb, qi, ki, wi_t)
    merged = _merge(out_a, out_b, ga, gb, w_pa, w_pb, tm)
    h, hm = _outproj(merged, w_out, x, g1, norm_ffn_g, sc2, sh2, tm, tiles_per_group)
    qp = _linear(hm, w_qp, tm)
    tm_peer = min(r, 512)
    e1n, e2, tau = _peer_route(qp, keys1, keys2, min(r, 256))
    peer = _peer_dense(hm, pu, pv, e1n, e2, tau, tm_peer, 512)
    y = _residual(h, peer, g2, tm, tiles_per_group)
    return y, (ka, va, kb, vb, ki)


def kernel(x_prompt, x_sample, cache_k_a, cache_v_a, cache_k_b, cache_v_b, cache_k_idx, page_table, c_prompt, c_sample, rel_bias, w_ada, b_ada, norm_mix_g, w_in, qnorm_a, knorm_a, qnorm_b, knorm_b, w_proj_a, w_proj_b, w_out, norm_ffn_g, w_q_peer, peer_keys1, peer_keys2, peer_u, peer_v):
    batch, seq, d = x_prompt.shape
    ns, dec_seq, _ = x_sample.shape
    depth = w_ada.shape[0]
    assert depth == 1 and dec_seq == 1 and seq % ATT_TILE == 0
    n_heads = d // 256
    n_idx_heads = d // 128
    w_att = n_heads * HEAD_DIM
    n_pages = page_table.shape[1]
    past = n_pages * PAGE_SIZE
    ns_pad = -(-ns // LANES) * LANES

    n_c = batch + ns_pad
    n_c_pad = -(-n_c // 8) * 8
    c_all = jnp.zeros((n_c_pad, d), F32).at[:batch].set(c_prompt).at[batch:batch + ns].set(c_sample)
    mod = _ada_mod(c_all, w_ada[0], b_ada[0])
    mods_p = tuple(mod[:batch, i * d:(i + 1) * d].reshape(batch, 1, d) for i in range(6))
    mods_s = tuple(mod[batch:batch + ns_pad, i * d:(i + 1) * d].reshape(1, ns_pad, d) for i in range(6))

    lw = (norm_mix_g[0], _split_weights(w_in[0], w_att, w_att, n_idx_heads, d),
          qnorm_a[0], knorm_a[0], qnorm_b[0], knorm_b[0],
          w_proj_a[0].astype(BF16), w_proj_b[0].astype(BF16), w_out[0].astype(BF16), norm_ffn_g[0],
          w_q_peer[0].astype(BF16), peer_keys1[0], peer_keys2[0],
          peer_u[0].astype(BF16), peer_v[0].astype(BF16))
    rel_bias_t = rel_bias.T
    tables = _bias_tables(rel_bias_t, seq)

    def attend_prompt(qa, ka, va, qb, kb, vb, qi, ki, wi_t):
        out_a = _moba_prompt(qa, ka, va, tables, batch, seq, n_heads)
        mask = _dsa_index(qi, wi_t, ki, batch, seq, n_idx_heads)
        out_b = _dsa_prompt(qb, kb, vb, tables, mask, batch, seq, n_heads, n_heads)
        return out_a, out_b

    tm_p = 512 if seq % 512 == 0 else ATT_TILE
    y_p, new_p = _layer_rows(x_prompt.reshape(batch * seq, d), mods_p, tm_p, seq // tm_p, lw, attend_prompt)

    def attend_sample(qa, ka, va, qb, kb, vb, qi, ki, wi_t):
        hd = lambda a: a[:ns].reshape(ns, n_heads, HEAD_DIM)
        wi = wi_t[:, :ns].T.reshape(ns, n_idx_heads, 1)
        blk, score = _sample_scan(page_table, cache_k_a, cache_k_idx, hd(qa),
                                  qi[:ns].reshape(ns, n_idx_heads, IDX_DIM), wi,
                                  ki[:ns].reshape(ns, 1, IDX_DIM), n_heads, n_idx_heads)
        blk_flat = blk[:, :MOBA_TOPK, :n_heads].transpose(0, 2, 1).reshape(ns, n_heads * MOBA_TOPK)
        out_a = _sample_moba(page_table, blk_flat, rel_bias_t, hd(qa), cache_k_a, cache_v_a, hd(ka), hd(va),
                             n_heads)
        n_sel = min(DSA_TOPK, (past + 1) // 4)
        sel = _sample_topk(score.reshape(ns, -1), n_sel, past + 1)
        out_b = _sample_dsa(page_table, sel, rel_bias_t, hd(qb), hd(kb), hd(vb), cache_k_b, cache_v_b,
                            n_heads, n_heads)
        pad = lambda a: jnp.zeros((ns_pad, w_att), F32).at[:ns].set(a.reshape(ns, w_att))
        return pad(out_a), pad(out_b)

    x_s = jnp.zeros((ns_pad, d), F32).at[:ns].set(x_sample.reshape(ns, d))
    y_s, new_s = _layer_rows(x_s, mods_s, ns_pad, 1, lw, attend_sample)

    def state(new, n_seq, t):
        ka, va, kb, vb, ki = new
        hd = lambda a: a[:n_seq * t].reshape(1, n_seq, t, n_heads, HEAD_DIM)
        return hd(ka), hd(va), hd(kb), hd(vb), ki[:n_seq * t].reshape(1, n_seq, t, IDX_DIM)

    return ((y_p.reshape(batch, seq, d), y_s[:ns].reshape(ns, 1, d))
            + state(new_p, batch, seq) + state(new_s, ns, 1))
```
